```python
import jax, jax.numpy as jnp
from jax import lax
import numpy as np

D_MODEL = 1024
BATCH = 8
SEQ = 4096
DEPTH = 4

CHUNK = 64
SUB = 16
N_SUB = CHUNK // SUB
EPS = 1e-6

A_HEADS = 8
A_DK = 128
A_KEY = A_HEADS * A_DK
A_DV = D_MODEL // A_HEADS
B_HEADS = 4
B_KEY = D_MODEL // 2
B_DK = B_KEY // B_HEADS
B_DV = D_MODEL // B_HEADS
B_RANK = 16
B_GATE_NORM = 16.0
C_HEADS = 8
C_DK = 128
C_KEY = C_HEADS * C_DK
C_DV = D_MODEL // C_HEADS
C_CONV = 4
C_QKV = 2 * C_KEY + D_MODEL
D_FF = 4 * D_MODEL
N_BRANCH = 3

SPLITS = (A_KEY, A_KEY, D_MODEL, D_MODEL,
          B_KEY, B_KEY, D_MODEL, B_RANK, D_MODEL,
          C_QKV, C_HEADS, C_HEADS, D_MODEL,
          N_BRANCH * D_MODEL)
N_IN = 2 * A_KEY + 2 * D_MODEL + 2 * B_KEY + 2 * D_MODEL + B_RANK + C_QKV + 2 * C_HEADS + D_MODEL + N_BRANCH * D_MODEL

kernel_name = 'hybrid_hgrn2_gla_gdn_encoder'


def rmsnorm(x, w):
    xf = x.astype(jnp.float32)
    y = xf * lax.rsqrt(jnp.mean(xf * xf, axis=-1, keepdims=True) + EPS)
    return (y * w.astype(jnp.float32)).astype(x.dtype)


def l2norm(x):
    return x * lax.rsqrt(jnp.sum(x * x, axis=-1, keepdims=True) + EPS)


def to_chunks(t):
    b, t_len, h, d = t.shape
    return t.reshape(b, t_len // CHUNK, CHUNK, h, d).transpose(1, 0, 3, 2, 4)


def from_chunks(t):
    nc, b, h, c, d = t.shape
    return t.transpose(1, 0, 3, 2, 4).reshape(b, nc * c, h, d)


def scalar_chunks(t):
    b, t_len, h = t.shape
    return t.reshape(b, t_len // CHUNK, CHUNK, h).transpose(1, 0, 3, 2)


def gated_linear_attention(q, k, v, log_f):
    bsz, _, h, kd = q.shape
    vd = v.shape[-1]
    qc, kc, vc, gc = (to_chunks(t.astype(jnp.float32)) for t in (q, k, v, log_f))
    tri = jnp.tril(jnp.ones((SUB, SUB), bool))
    strict_sub = jnp.tril(jnp.ones((N_SUB, N_SUB), bool), -1)
    eye_sub = jnp.eye(N_SUB, dtype=jnp.float32)

    def step(state, inp):
        qi, ki, vi, gi = inp
        b = jnp.cumsum(gi, axis=-2)
        o_inter = jnp.einsum('bhck,bhkv->bhcv', qi * jnp.exp(b), state)
        qs, ks, bs = (t.reshape(bsz, h, N_SUB, SUB, kd) for t in (qi, ki, b))
        b_excl = (b - gi).reshape(bsz, h, N_SUB, SUB, kd)
        diff = jnp.where(tri[:, :, None], bs[..., :, None, :] - bs[..., None, :, :], -jnp.inf)
        a_diag = jnp.einsum('bhnik,bhnijk,bhnjk->bhnij', qs, jnp.exp(diff), ks)
        b_start = b_excl[..., 0, :]
        b_end = bs[..., -1, :]
        q_off = qs * jnp.exp(bs - b_start[..., None, :])
        k_off = ks * jnp.exp(b_end[..., None, :] - bs)
        pair = jnp.where(strict_sub[:, :, None],
                         b_start[:, :, :, None, :] - b_end[:, :, None, :, :], -jnp.inf)
        a_off = jnp.einsum('bhsik,bhstk,bhtjk->bhsitj', q_off, jnp.exp(pair), k_off)
        a = (a_off + jnp.einsum('bhnij,nm->bhnimj', a_diag, eye_sub)).reshape(bsz, h, CHUNK, CHUNK)
        o = o_inter + jnp.einsum('bhij,bhjv->bhiv', a, vi)
        b_last = b[..., -1, :]
        state = jnp.exp(b_last)[..., None] * state + jnp.einsum(
            'bhck,bhcv->bhkv', ki * jnp.exp(b_last[..., None, :] - b), vi)
        return state, o

    s0 = jnp.zeros((bsz, h, kd, vd), jnp.float32)
    _, o = lax.scan(step, s0, (qc, kc, vc, gc))
    return from_chunks(o)


def gated_delta_rule(q, k, v, log_a, beta):
    bsz, _, h, kd = q.shape
    vd = v.shape[-1]
    qc, kc, vc = (to_chunks(t.astype(jnp.float32)) for t in (q, k, v))
    gc = scalar_chunks(log_a.astype(jnp.float32))
    bc = scalar_chunks(beta.astype(jnp.float32))
    b = jnp.cumsum(gc, axis=-1)
    causal = jnp.tril(jnp.ones((CHUNK, CHUNK), bool))
    strict = jnp.tril(jnp.ones((CHUNK, CHUNK), bool), -1)
    decay = jnp.exp(jnp.where(causal, b[..., :, None] - b[..., None, :], -jnp.inf))
    kk = jnp.einsum('nbhik,nbhjk->nbhij', kc, kc)
    m = jnp.eye(CHUNK, dtype=jnp.float32) + jnp.where(strict, bc[..., :, None] * kk * decay, 0.0)
    rhs = jnp.concatenate([vc * bc[..., None], kc * (bc * jnp.exp(b))[..., None]], axis=-1)
    sol = lax.linalg.triangular_solve(m, rhs, left_side=True, lower=True, unit_diagonal=True)
    u, w = sol[..., :vd], sol[..., vd:]
    qk = jnp.einsum('nbhik,nbhjk->nbhij', qc, kc) * decay
    q_dec = qc * jnp.exp(b)[..., None]
    k_dec = kc * jnp.exp(b[..., -1:] - b)[..., None]
    a_last = jnp.exp(b[..., -1])

    def step(state, inp):
        u_i, w_i, qk_i, qd_i, kd_i, al_i = inp
        v_new = u_i - jnp.einsum('bhck,bhkv->bhcv', w_i, state)
        o = jnp.einsum('bhck,bhkv->bhcv', qd_i, state) + jnp.einsum('bhij,bhjv->bhiv', qk_i, v_new)
        state = al_i[..., None, None] * state + jnp.einsum('bhck,bhcv->bhkv', kd_i, v_new)
        return state, o

    s0 = jnp.zeros((bsz, h, kd, vd), jnp.float32)
    _, o = lax.scan(step, s0, (u, w, qk, q_dec, k_dec, a_last))
    return from_chunks(o)


def causal_short_conv(x, w):
    kw = w.shape[0]
    t_len = x.shape[1]
    xp = jnp.pad(x, ((0, 0), (kw - 1, 0), (0, 0)))
    out = xp[:, 0:t_len] * w[0]
    for j in range(1, kw):
        out = out + xp[:, j:j + t_len] * w[j]
    return out


def hybrid_mixer(h, w_in, lb, gla_w_gk, gla_b_gk, gdn_conv, gdn_a_log, gdn_dt_bias,
                 hgrn_onorm, gla_onorm, gdn_onorm, w_out):
    bsz, t_len, _ = h.shape
    proj = h @ w_in
    cuts = [int(c) for c in np.cumsum(SPLITS)[:-1]]
    (a_q, a_f, a_i, a_g, b_q, b_k, b_v, b_gk, b_g,
     c_qkv, c_a, c_b, c_g, merge) = jnp.split(proj, cuts, axis=-1)

    z = a_f.astype(jnp.float32).reshape(bsz, t_len, A_HEADS, A_DK)
    lb_h = lb.reshape(A_HEADS, A_DK)
    log_f_a = jnp.logaddexp(jnp.log(lb_h), jnp.log1p(-lb_h) + jax.nn.log_sigmoid(z))
    k_a = (1.0 - lb_h) * jax.nn.sigmoid(-z)
    q_a = jax.nn.silu(a_q).reshape(bsz, t_len, A_HEADS, A_DK)
    o_a = gated_linear_attention(q_a, k_a, a_i.reshape(bsz, t_len, A_HEADS, A_DV), log_f_a)
    y_a = (rmsnorm(o_a, hgrn_onorm) * jax.nn.silu(a_g.reshape(bsz, t_len, A_HEADS, A_DV))).reshape(bsz, t_len, D_MODEL)

    q_b = b_q.reshape(bsz, t_len, B_HEADS, B_DK) * (B_DK ** -0.5)
    k_b = b_k.reshape(bsz, t_len, B_HEADS, B_DK)
    gk = (b_gk @ gla_w_gk + gla_b_gk).astype(jnp.float32)
    log_f_b = (jax.nn.log_sigmoid(gk) / B_GATE_NORM).reshape(bsz, t_len, B_HEADS, B_DK)
    o_b = gated_linear_attention(q_b, k_b, b_v.reshape(bsz, t_len, B_HEADS, B_DV), log_f_b)
    y_b = (rmsnorm(o_b, gla_onorm) * jax.nn.silu(b_g.reshape(bsz, t_len, B_HEADS, B_DV))).reshape(bsz, t_len, D_MODEL)

    qkv = jax.nn.silu(causal_short_conv(c_qkv, gdn_conv))
    q_c = l2norm(qkv[..., :C_KEY].astype(jnp.float32).reshape(bsz, t_len, C_HEADS, C_DK)) * (C_DK ** -0.5)
    k_c = l2norm(qkv[..., C_KEY:2 * C_KEY].astype(jnp.float32).reshape(bsz, t_len, C_HEADS, C_DK))
    v_c = qkv[..., 2 * C_KEY:].reshape(bsz, t_len, C_HEADS, C_DV)
    log_a_c = -jnp.exp(gdn_a_log.astype(jnp.float32)) * jax.nn.softplus((c_a + gdn_dt_bias).astype(jnp.float32))
    beta_c = jax.nn.sigmoid(c_b.astype(jnp.float32))
    o_c = gated_delta_rule(q_c, k_c, v_c, log_a_c, beta_c)
    y_c = (rmsnorm(o_c, gdn_onorm) * jax.nn.silu(c_g.reshape(bsz, t_len, C_HEADS, C_DV))).reshape(bsz, t_len, D_MODEL)

    g = jax.nn.sigmoid(merge.astype(jnp.float32)).reshape(bsz, t_len, N_BRANCH, D_MODEL)
    y = g[:, :, 0] * y_a + g[:, :, 1] * y_b + g[:, :, 2] * y_c
    return (y.astype(h.dtype) @ w_out).astype(h.dtype)


def squared_relu_mlp(h, w_up, w_down):
    return jnp.square(jax.nn.relu(h @ w_up)) @ w_down


def setup_inputs(seed: int = 0) -> dict:
    key = jax.random.key(seed)
    ks = jax.random.split(key, 20)
    f32 = jnp.float32

    def gain(k, n):
        return 1.0 + 0.02 * jax.random.normal(k, (DEPTH, n), f32)

    dt = jnp.exp(jax.random.uniform(ks[11], (DEPTH, C_HEADS), f32, np.log(1e-3), np.log(1e-1)))
    return {
        'x': jax.random.normal(ks[0], (BATCH, SEQ, D_MODEL), f32),
        'ln_mix_pre': gain(ks[1], D_MODEL),
        'ln_mix_post': gain(ks[2], D_MODEL),
        'ln_mlp_pre': gain(ks[3], D_MODEL),
        'ln_mlp_post': gain(ks[4], D_MODEL),
        'w_in': jax.random.normal(ks[5], (DEPTH, D_MODEL, N_IN), f32) * D_MODEL ** -0.5,
        'hgrn_lb_logits': 0.1 * jax.random.normal(ks[6], (DEPTH, A_KEY), f32),
        'gla_w_gk': jax.random.normal(ks[7], (DEPTH, B_RANK, B_KEY), f32) * B_RANK ** -0.5,
        'gla_b_gk': 0.1 * jax.random.normal(ks[8], (DEPTH, B_KEY), f32),
        'gdn_conv': jax.random.normal(ks[9], (DEPTH, C_CONV, C_QKV), f32) * C_CONV ** -0.5,
        'gdn_a_log': jnp.log(jax.random.uniform(ks[10], (DEPTH, C_HEADS), f32, 1.0, 16.0)),
        'gdn_dt_bias': dt + jnp.log(-jnp.expm1(-dt)),
        'hgrn_onorm': gain(ks[12], A_DV),
        'gla_onorm': gain(ks[13], B_DV),
        'gdn_onorm': gain(ks[14], C_DV),
        'w_out': jax.random.normal(ks[15], (DEPTH, D_MODEL, D_MODEL), f32) * D_MODEL ** -0.5,
        'w_up': jax.random.normal(ks[16], (DEPTH, D_MODEL, D_FF), f32) * D_MODEL ** -0.5,
        'w_down': jax.random.normal(ks[17], (DEPTH, D_FF, D_MODEL), f32) * D_FF ** -0.5,
    }


def reference(x, ln_mix_pre, ln_mix_post, ln_mlp_pre, ln_mlp_post, w_in, hgrn_lb_logits,
              gla_w_gk, gla_b_gk, gdn_conv, gdn_a_log, gdn_dt_bias, hgrn_onorm, gla_onorm,
              gdn_onorm, w_out, w_up, w_down):
    lb_cum = jnp.cumsum(jax.nn.softmax(hgrn_lb_logits.astype(jnp.float32), axis=0), axis=0)
    for layer in range(DEPTH):
        lb = jnp.clip(lb_cum[layer] - lb_cum[0], 0.0, 1.0)
        h = rmsnorm(x, ln_mix_pre[layer])
        mix = hybrid_mixer(h, w_in[layer], lb, gla_w_gk[layer], gla_b_gk[layer], gdn_conv[layer],
                           gdn_a_log[layer], gdn_dt_bias[layer], hgrn_onorm[layer], gla_onorm[layer],
                           gdn_onorm[layer], w_out[layer])
        x = x + rmsnorm(mix, ln_mix_post[layer]).astype(x.dtype)
        h = rmsnorm(x, ln_mlp_pre[layer])
        x = x + rmsnorm(squared_relu_mlp(h, w_up[layer], w_down[layer]), ln_mlp_post[layer]).astype(x.dtype)
    return x
```

```python
import functools

import jax
import jax.numpy as jnp
from jax import lax
from jax.experimental import pallas as pl
from jax.experimental.pallas import tpu as pltpu

F32 = jnp.float32
BF16 = jnp.bfloat16

D_MODEL = 1024
DEPTH = 4
CHUNK = 64
SUB = 16
N_SUB = CHUNK // SUB
EPS = 1e-6
LANES = 128
SUBLANES = 8

A_HEADS, A_DK, A_DV = 8, 128, 128
B_HEADS, B_DK, B_DV = 4, 128, 256
B_KEY = B_HEADS * B_DK
B_RANK = 16
B_GATE_NORM = 16.0
C_HEADS, C_DK, C_DV = 8, 128, 128
C_KEY = C_HEADS * C_DK
C_CONV = 4
C_QKV = 2 * C_KEY + D_MODEL
D_FF = 4 * D_MODEL

BLK_AQ, BLK_AF, BLK_AI, BLK_AG = 0, 1, 2, 3
BLK_BQK, BLK_BV, BLK_BG = 4, 5, 6
BLK_CQ, BLK_CK, BLK_CV, BLK_CG = 7, 8, 9, 10
BLK_M0, BLK_M1, BLK_M2 = 11, 12, 13
N_MAIN = 14 * D_MODEL
SM_GK, SM_A, SM_B = 0, B_RANK, B_RANK + C_HEADS

VMEM_LIMIT = 56 * 1024 * 1024


def _dot(a, b):
    return jnp.dot(a, b, preferred_element_type=F32)


def _dot_nt(a, b):
    return lax.dot_general(a, b, (((1,), (1,)), ((), ())), preferred_element_type=F32)


def _dot_tn(a, b):
    return lax.dot_general(a, b, (((0,), (0,)), ((), ())), preferred_element_type=F32)


def _split2(a):
    hi = a.astype(BF16)
    lo = (a - hi.astype(F32)).astype(BF16)
    return hi, lo


def _dot3(a, b):
    ah, al = _split2(a)
    bh, bl = _split2(b)
    return _dot(ah, bh) + _dot(ah, bl) + _dot(al, bh)


def _cumsum_rows(g):
    row = lax.broadcasted_iota(jnp.int32, (CHUNK, CHUNK), 0)
    col = lax.broadcasted_iota(jnp.int32, (CHUNK, CHUNK), 1)
    tril = jnp.where(row >= col, 1.0, 0.0).astype(BF16)
    g1 = g.astype(BF16)
    r1 = g - g1.astype(F32)
    g2 = r1.astype(BF16)
    g3 = (r1 - g2.astype(F32)).astype(BF16)
    return _dot(tril, g1) + _dot(tril, g2) + _dot(tril, g3)


def _sigmoid(x):
    return jax.nn.sigmoid(x)


def _silu(x):
    return x * jax.nn.sigmoid(x)


def _log_sigmoid(x):
    return jnp.minimum(x, 0.0) - jnp.log1p(jnp.exp(-jnp.abs(x)))


def _rms_rows(x, w):
    return x * lax.rsqrt(jnp.mean(x * x, axis=-1, keepdims=True) + EPS) * w


def _lb_kernel(logits_ref, out_ref):
    lg = logits_ref[...]
    e = jnp.exp(lg - jnp.max(lg, axis=0, keepdims=True))
    p = e / jnp.sum(e, axis=0, keepdims=True)
    cum = p[0:1]
    first = cum
    pad = jnp.zeros((SUBLANES - 3, lg.shape[1]), F32)
    for layer in range(DEPTH):
        if layer > 0:
            cum = cum + p[layer:layer + 1]
        lb = jnp.clip(cum - first, 0.0, 1.0)
        out_ref[layer] = jnp.concatenate([jnp.log(lb), jnp.log1p(-lb), 1.0 - lb, pad], axis=0)


def _lb_params(logits):
    return pl.pallas_call(
        _lb_kernel,
        out_shape=jax.ShapeDtypeStruct((DEPTH, SUBLANES, logits.shape[1]), F32),
        name="hgrn_lower_bounds",
    )(logits.astype(F32))


PROJ_TM = 512
PROJ_TN = 1024


def _proj_kernel(x_ref, lnw_ref, w_ref, ws_ref, o_ref, os_ref, h_ref):
    @pl.when(pl.program_id(1) == 0)
    def _():
        h = _rms_rows(x_ref[...], lnw_ref[...])
        hh, hl = _split2(h)
        h_ref[...] = hh
        wh, wl = _split2(ws_ref[...])
        os_ref[...] = _dot(hh, wh) + _dot(hh, wl) + _dot(hl, wh)

    o_ref[...] = _dot(h_ref[...], w_ref[...])


def _proj(x2, lnw, w_main, w_small):
    n = x2.shape[0]
    tm = min(PROJ_TM, n)
    return pl.pallas_call(
        _proj_kernel,
        grid=(n // tm, N_MAIN // PROJ_TN),
        in_specs=[
            pl.BlockSpec((tm, D_MODEL), lambda i, j: (i, 0)),
            pl.BlockSpec((1, D_MODEL), lambda i, j: (0, 0)),
            pl.BlockSpec((D_MODEL, PROJ_TN), lambda i, j: (0, j)),
            pl.BlockSpec((D_MODEL, LANES), lambda i, j: (0, 0)),
        ],
        out_specs=[
            pl.BlockSpec((tm, PROJ_TN), lambda i, j: (i, j)),
            pl.BlockSpec((tm, LANES), lambda i, j: (i, 0)),
        ],
        out_shape=[
            jax.ShapeDtypeStruct((n, N_MAIN), F32),
            jax.ShapeDtypeStruct((n, LANES), F32),
        ],
        scratch_shapes=[pltpu.VMEM((tm, D_MODEL), BF16)],
        compiler_params=pltpu.CompilerParams(
            dimension_semantics=("arbitrary", "arbitrary"), vmem_limit_bytes=VMEM_LIMIT),
        name="rmsnorm_in_proj",
    )(x2, lnw, w_main, w_small)


def _gla_chunk(q, k, v, g, st_ref, kb_ref, heads, dk, dv):
    b = _cumsum_rows(g)
    b_last = b[CHUNK - 1:CHUNK]
    qd = (q * jnp.exp(b)).astype(BF16)
    kd = (k * jnp.exp(b_last - b)).astype(BF16)
    s_decay = jnp.exp(b_last)
    vb = v.astype(BF16)

    kb_ref[0] = k
    kb_ref[1] = b

    q_off, k_off = [], []
    for n in range(1, N_SUB):
        lo = n * SUB
        b_start = b[lo:lo + 1] - g[lo:lo + 1]
        q_off.append((q[lo:lo + SUB] * jnp.exp(b[lo:lo + SUB] - b_start)).astype(BF16))
        k_off.append((k * jnp.exp(jnp.minimum(b_start - b, 0.0))).astype(BF16))

    lane = lax.broadcasted_iota(jnp.int32, (SUBLANES, CHUNK), 1)
    n_piece = CHUNK // SUBLANES
    a_diag = [[jnp.zeros((SUBLANES, CHUNK), F32) for _ in range(n_piece)] for _ in range(heads)]
    for r in range(CHUNK):
        n, j = divmod(r, SUB)
        first_piece = (n * SUB + (SUBLANES if j >= SUBLANES else 0)) // SUBLANES
        last_piece = (n * SUB + SUB) // SUBLANES
        lo, hi = first_piece * SUBLANES, last_piece * SUBLANES
        k_r = kb_ref[0, pl.ds(r, 1), :]
        b_r = kb_ref[1, pl.ds(r, 1), :]
        p = q[lo:hi] * (k_r * jnp.exp(jnp.minimum(b[lo:hi] - b_r, 0.0)))
        for h in range(heads):
            col = jnp.sum(p[:, h * dk:(h + 1) * dk], axis=-1, keepdims=True)
            for m in range(first_piece, last_piece):
                c = col[(m - first_piece) * SUBLANES:(m - first_piece + 1) * SUBLANES]
                a_diag[h][m] = jnp.where(lane == r, c, a_diag[h][m])

    row = lax.broadcasted_iota(jnp.int32, (SUB, CHUNK), 0)
    colid = lax.broadcasted_iota(jnp.int32, (SUB, CHUNK), 1)
    outs = []
    for h in range(heads):
        ks, vs = slice(h * dk, (h + 1) * dk), slice(h * dv, (h + 1) * dv)
        rows = []
        for n in range(N_SUB):
            diag = jnp.concatenate(a_diag[h][2 * n:2 * n + 2], axis=0)
            a_n = jnp.where(colid <= row + n * SUB, diag, 0.0)
            if n > 0:
                off = _dot_nt(q_off[n - 1][:, ks], k_off[n - 1][:, ks])
                a_n = jnp.where(colid < n * SUB, off, a_n)
            rows.append(a_n)
        a = jnp.concatenate(rows, axis=0).astype(BF16)
        st = st_ref[h]
        o = _dot_nt(qd[:, ks], st.astype(BF16)) + _dot(a, vb[:, vs])
        st_ref[h] = st * s_decay[:, ks] + _dot_tn(vb[:, vs], kd[:, ks])
        outs.append(o)
    return outs


def _head_out(outs, onorm, gate, merge):
    y = jnp.concatenate([_rms_rows(o, onorm) for o in outs], axis=-1)
    return y * _silu(gate) * _sigmoid(merge)


def _mixer_a_kernel(aq_ref, af_ref, ai_ref, ag_ref, m_ref, lbp_ref, onorm_ref, y_ref, st_ref, kb_ref):
    @pl.when(pl.program_id(1) == 0)
    def _():
        st_ref[...] = jnp.zeros_like(st_ref)

    z = af_ref[...]
    log_lb, log_1m_lb, one_m_lb = lbp_ref[0:1, :], lbp_ref[1:2, :], lbp_ref[2:3, :]
    t = log_1m_lb + _log_sigmoid(z)
    g = jnp.maximum(log_lb, t) + jnp.log1p(jnp.exp(-jnp.abs(log_lb - t)))
    k = one_m_lb * _sigmoid(-z)
    q = _silu(aq_ref[...])
    outs = _gla_chunk(q, k, ai_ref[...], g, st_ref, kb_ref, A_HEADS, A_DK, A_DV)
    y_ref[...] = _head_out(outs, onorm_ref[...], ag_ref[...], m_ref[...])


def _mixer_b_kernel(qk_ref, v_ref, bg_ref, m_ref, sm_ref, wgk_ref, bgk_ref, onorm_ref, y_ref, st_ref, kb_ref):
    @pl.when(pl.program_id(1) == 0)
    def _():
        st_ref[...] = jnp.zeros_like(st_ref)

    qk = qk_ref[...]
    q = qk[:, :B_KEY] * (B_DK ** -0.5)
    k = qk[:, B_KEY:]
    gk = _dot3(sm_ref[...], wgk_ref[...]) + bgk_ref[...]
    g = _log_sigmoid(gk) / B_GATE_NORM
    outs = _gla_chunk(q, k, v_ref[...], g, st_ref, kb_ref, B_HEADS, B_DK, B_DV)
    y_ref[...] = _head_out(outs, onorm_ref[...], bg_ref[...], m_ref[...])


def _tok_spec(blk, width=D_MODEL):
    return pl.BlockSpec((None, CHUNK, width), lambda b, t: (b, t, blk))


def _full_spec(shape):
    return pl.BlockSpec(shape, lambda b, t: (0,) * len(shape))


def _mixer_call(kernel, name, proj3, tok_blocks, extra_inputs, extra_specs, scratch):
    bsz, t_len, _ = proj3.shape
    return pl.pallas_call(
        kernel,
        grid=(bsz, t_len // CHUNK),
        in_specs=[_tok_spec(blk) for blk in tok_blocks] + extra_specs,
        out_specs=pl.BlockSpec((None, CHUNK, D_MODEL), lambda b, t: (b, t, 0)),
        out_shape=jax.ShapeDtypeStruct((bsz, t_len, D_MODEL), F32),
        scratch_shapes=scratch,
        compiler_params=pltpu.CompilerParams(
            dimension_semantics=("arbitrary", "arbitrary"), vmem_limit_bytes=VMEM_LIMIT),
        name=name,
    )(*([proj3] * len(tok_blocks)), *extra_inputs)


def _mixer_a(proj3, lbp, onorm):
    return _mixer_call(
        _mixer_a_kernel, "hgrn2_mixer", proj3,
        [BLK_AQ, BLK_AF, BLK_AI, BLK_AG, BLK_M0],
        [lbp, onorm],
        [_full_spec((SUBLANES, D_MODEL)), _full_spec((1, A_DV))],
        [pltpu.VMEM((A_HEADS, A_DV, A_DK), F32), pltpu.VMEM((2, CHUNK, A_HEADS * A_DK), F32)])


def _mixer_b(proj3, small3, wgk, bgk, onorm):
    return _mixer_call(
        _mixer_b_kernel, "gla_mixer", proj3,
        [BLK_BQK, BLK_BV, BLK_BG, BLK_M1],
        [small3, wgk, bgk, onorm],
        [pl.BlockSpec((None, CHUNK, LANES), lambda b, t: (b, t, 0)),
         _full_spec((LANES, B_KEY)), _full_spec((1, B_KEY)), _full_spec((1, B_DV))],
        [pltpu.VMEM((B_HEADS, B_DV, B_DK), F32), pltpu.VMEM((2, CHUNK, B_KEY), F32)])


def _l2norm_rows(x):
    return x * lax.rsqrt(jnp.sum(x * x, axis=-1, keepdims=True) + EPS)


def _mixer_c_kernel(cq_ref, ck_ref, cv_ref, cg_ref, m_ref, sm_ref, conv_ref, hp_ref, onorm_ref,
                    y_ref, st_ref, xb_ref):
    hist = SUBLANES

    @pl.when(pl.program_id(1) == 0)
    def _():
        st_ref[...] = jnp.zeros_like(st_ref)
        xb_ref[0:hist, :] = jnp.zeros((hist, C_QKV), F32)

    xb_ref[hist:hist + CHUNK, 0:C_KEY] = cq_ref[...]
    xb_ref[hist:hist + CHUNK, C_KEY:2 * C_KEY] = ck_ref[...]
    xb_ref[hist:hist + CHUNK, 2 * C_KEY:C_QKV] = cv_ref[...]
    acc = xb_ref[hist:hist + CHUNK, :] * conv_ref[C_CONV - 1:C_CONV, :]
    for j in range(C_CONV - 1):
        shift = C_CONV - 1 - j
        acc = acc + xb_ref[hist - shift:hist - shift + CHUNK, :] * conv_ref[j:j + 1, :]
    xb_ref[0:hist, :] = xb_ref[CHUNK:CHUNK + hist, :]
    qkv = _silu(acc)

    sm = sm_ref[...]
    log_a = -jnp.exp(hp_ref[0:1, :]) * jax.nn.softplus(sm + hp_ref[1:2, :])
    beta_all = _sigmoid(sm)
    b_all = _cumsum_rows(log_a)
    b_all_t = b_all.T

    row = lax.broadcasted_iota(jnp.int32, (CHUNK, CHUNK), 0)
    col = lax.broadcasted_iota(jnp.int32, (CHUNK, CHUNK), 1)
    eye = jnp.where(row == col, 1.0, 0.0).astype(F32)

    outs = []
    for h in range(C_HEADS):
        hs = slice(h * C_DK, (h + 1) * C_DK)
        q = _l2norm_rows(qkv[:, hs]) * (C_DK ** -0.5)
        k = _l2norm_rows(qkv[:, C_KEY + h * C_DK:C_KEY + (h + 1) * C_DK])
        v = qkv[:, 2 * C_KEY + h * C_DV:2 * C_KEY + (h + 1) * C_DV]
        b_col = b_all[:, SM_A + h:SM_A + h + 1]
        b_row = b_all_t[SM_A + h:SM_A + h + 1, :]
        beta = beta_all[:, SM_B + h:SM_B + h + 1]
        b_last = b_col[CHUNK - 1:CHUNK]
        decay = jnp.where(row >= col, jnp.exp(jnp.minimum(b_col - b_row, 0.0)), 0.0)
        kb = k.astype(BF16)
        kk = _dot_nt(kb, kb)
        lower = jnp.where(row > col, beta * kk * decay, 0.0)
        x = lower
        t_inv = eye - lower
        for _ in range(5):
            x = _dot3(x, x)
            t_inv = t_inv + _dot3(t_inv, x)
        e_b = jnp.exp(b_col)
        rhs = jnp.concatenate([v * beta, k * (beta * e_b)], axis=-1).astype(BF16)
        sol = _dot(t_inv.astype(BF16), rhs)
        u, w = sol[:, :C_DV], sol[:, C_DV:]
        qb = q.astype(BF16)
        qk = _dot_nt(qb, kb) * decay
        st = st_ref[h]
        stb = st.astype(BF16)
        v_new = u - _dot_nt(w.astype(BF16), stb)
        o = _dot_nt((q * e_b).astype(BF16), stb) + _dot(qk.astype(BF16), v_new.astype(BF16))
        k_dec = (k * jnp.exp(b_last - b_col)).astype(BF16)
        st_ref[h] = jnp.exp(b_last) * st + _dot_tn(v_new.astype(BF16), k_dec)
        outs.append(o)
    y_ref[...] = _head_out(outs, onorm_ref[...], cg_ref[...], m_ref[...])


def _mixer_c(proj3, small3, conv, head_params, onorm):
    return _mixer_call(
        _mixer_c_kernel, "gated_deltanet_mixer", proj3,
        [BLK_CQ, BLK_CK, BLK_CV, BLK_CG, BLK_M2],
        [small3, conv, head_params, onorm],
        [pl.BlockSpec((None, CHUNK, LANES), lambda b, t: (b, t, 0)),
         _full_spec((C_CONV, C_QKV)), _full_spec((SUBLANES, LANES)), _full_spec((1, C_DV))],
        [pltpu.VMEM((C_HEADS, C_DV, C_DK), F32), pltpu.VMEM((SUBLANES + CHUNK, C_QKV), F32)])


MLP_TM = 256


def _out_mlp_kernel(x_ref, ya_ref, yb_ref, yc_ref, wo_ref, wu_ref, wd_ref, ln_ref, o_ref):
    y = (ya_ref[...] + yb_ref[...] + yc_ref[...]).astype(BF16)
    mix = _dot(y, wo_ref[...])
    x1 = x_ref[...] + _rms_rows(mix, ln_ref[0:1, :])
    h = _rms_rows(x1, ln_ref[1:2, :]).astype(BF16)
    up = _dot(h, wu_ref[...])
    act = jnp.square(jnp.maximum(up, 0.0)).astype(BF16)
    down = _dot(act, wd_ref[...])
    o_ref[...] = x1 + _rms_rows(down, ln_ref[2:3, :])


def _out_mlp(x2, ya, yb, yc, wo, wu, wd, ln):
    n = x2.shape[0]
    tm = min(MLP_TM, n)
    tok = pl.BlockSpec((tm, D_MODEL), lambda i: (i, 0))

    def resident(shape):
        return pl.BlockSpec(shape, lambda i: (0, 0), pipeline_mode=pl.Buffered(1))

    return pl.pallas_call(
        _out_mlp_kernel,
        grid=(n // tm,),
        in_specs=[tok, tok, tok, tok,
                  resident((D_MODEL, D_MODEL)), resident((D_MODEL, D_FF)), resident((D_FF, D_MODEL)),
                  resident((SUBLANES, D_MODEL))],
        out_specs=tok,
        out_shape=jax.ShapeDtypeStruct((n, D_MODEL), F32),
        compiler_params=pltpu.CompilerParams(
            dimension_semantics=("arbitrary",), vmem_limit_bytes=VMEM_LIMIT),
        name="merge_out_proj_mlp",
    )(x2, ya, yb, yc, wo, wu, wd, ln)


def _reorder_w_in(w_in):
    o_bgk = 4 * D_MODEL + 2 * B_KEY + D_MODEL
    o_bg = o_bgk + B_RANK
    o_cqkv = o_bg + D_MODEL
    o_ca = o_cqkv + C_QKV
    o_cg = o_ca + 2 * C_HEADS
    main = jnp.concatenate([w_in[..., :o_bgk], w_in[..., o_bg:o_ca], w_in[..., o_cg:]], axis=-1)
    pad = jnp.zeros(w_in.shape[:-1] + (LANES - B_RANK - 2 * C_HEADS,), w_in.dtype)
    small = jnp.concatenate([w_in[..., o_bgk:o_bg], w_in[..., o_ca:o_cg], pad], axis=-1)
    return main.astype(BF16), small.astype(F32)


def kernel(x, ln_mix_pre, ln_mix_post, ln_mlp_pre, ln_mlp_post, w_in, hgrn_lb_logits, gla_w_gk, gla_b_gk,
           gdn_conv, gdn_a_log, gdn_dt_bias, hgrn_onorm, gla_onorm, gdn_onorm, w_out, w_up, w_down):
    bsz, t_len, _ = x.shape
    n = bsz * t_len
    assert t_len % CHUNK == 0 and n % min(PROJ_TM, n) == 0 and n % min(MLP_TM, n) == 0

    lbp = _lb_params(hgrn_lb_logits)
    w_main, w_small = _reorder_w_in(w_in)
    wgk = jnp.pad(gla_w_gk.astype(F32), ((0, 0), (0, LANES - B_RANK), (0, 0)))
    lane_pad = ((0, 0), (SM_A, LANES - SM_A - C_HEADS))
    head_params = jnp.stack([jnp.pad(gdn_a_log.astype(F32), lane_pad),
                             jnp.pad(gdn_dt_bias.astype(F32), lane_pad)], axis=1)
    head_params = jnp.pad(head_params, ((0, 0), (0, SUBLANES - 2), (0, 0)))
    ln_rest = jnp.stack([ln_mix_post, ln_mlp_pre, ln_mlp_post], axis=1).astype(F32)
    ln_rest = jnp.pad(ln_rest, ((0, 0), (0, SUBLANES - 3), (0, 0)))
    wo, wu, wd = w_out.astype(BF16), w_up.astype(BF16), w_down.astype(BF16)

    x2 = x.reshape(n, D_MODEL).astype(F32)
    for layer in range(DEPTH):
        proj, small = _proj(x2, ln_mix_pre[layer][None, :].astype(F32), w_main[layer], w_small[layer])
        proj3 = proj.reshape(bsz, t_len, N_MAIN)
        small3 = small.reshape(bsz, t_len, LANES)
        ya = _mixer_a(proj3, lbp[layer], hgrn_onorm[layer][None, :].astype(F32))
        yb = _mixer_b(proj3, small3, wgk[layer], gla_b_gk[layer][None, :].astype(F32),
                      gla_onorm[layer][None, :].astype(F32))
        yc = _mixer_c(proj3, small3, gdn_conv[layer].astype(F32), head_params[layer],
                      gdn_onorm[layer][None, :].astype(F32))
        x2 = _out_mlp(x2, ya.reshape(n, D_MODEL), yb.reshape(n, D_MODEL), yc.reshape(n, D_MODEL),
                      wo[layer], wu[layer], wd[layer], ln_rest[layer])
    return x2.reshape(bsz, t_len, D_MODEL).astype(x.dtype)
```

```python
import functools

import jax
import jax.numpy as jnp
from jax import lax
from jax.experimental import pallas as pl
from jax.experimental.pallas import tpu as pltpu

F32 = jnp.float32
BF16 = jnp.bfloat16

D_MODEL = 1024
DEPTH = 4
CHUNK = 64
SUB = 16
N_SUB = CHUNK // SUB
EPS = 1e-6
LANES = 128
SUBLANES = 8

A_HEADS, A_DK, A_DV = 8, 128, 128
B_HEADS, B_DK, B_DV = 4, 128, 256
B_KEY = B_HEADS * B_DK
B_RANK = 16
B_GATE_NORM = 16.0
C_HEADS, C_DK, C_DV = 8, 128, 128
C_KEY = C_HEADS * C_DK
C_CONV = 4
C_QKV = 2 * C_KEY + D_MODEL
D_FF = 4 * D_MODEL

BLK_AQ, BLK_AF, BLK_AI, BLK_AG = 0, 1, 2, 3
BLK_BQK, BLK_BV, BLK_BG = 4, 5, 6
BLK_CQ, BLK_CK, BLK_CV, BLK_CG = 7, 8, 9, 10
BLK_M0, BLK_M1, BLK_M2 = 11, 12, 13
N_MAIN = 14 * D_MODEL
SM_GK, SM_A, SM_B = 0, B_RANK, B_RANK + C_HEADS

VMEM_LIMIT = 56 * 1024 * 1024


def _dot(a, b):
    return jnp.dot(a, b, preferred_element_type=F32)


def _dot_nt(a, b):
    return lax.dot_general(a, b, (((1,), (1,)), ((), ())), preferred_element_type=F32)


def _dot_tn(a, b):
    return lax.dot_general(a, b, (((0,), (0,)), ((), ())), preferred_element_type=F32)


def _split2(a):
    hi = a.astype(BF16)
    lo = (a - hi.astype(F32)).astype(BF16)
    return hi, lo


def _dot3(a, b):
    ah, al = _split2(a)
    bh, bl = _split2(b)
    return _dot(ah, bh) + _dot(ah, bl) + _dot(al, bh)


def _cumsum_rows(g):
    row = lax.broadcasted_iota(jnp.int32, (CHUNK, CHUNK), 0)
    col = lax.broadcasted_iota(jnp.int32, (CHUNK, CHUNK), 1)
    tril = jnp.where(row >= col, 1.0, 0.0).astype(BF16)
    g1 = g.astype(BF16)
    r1 = g - g1.astype(F32)
    g2 = r1.astype(BF16)
    g3 = (r1 - g2.astype(F32)).astype(BF16)
    return _dot(tril, g1) + _dot(tril, g2) + _dot(tril, g3)


def _sigmoid(x):
    return jax.nn.sigmoid(x)


def _silu(x):
    return x * jax.nn.sigmoid(x)


def _log_sigmoid(x):
    return jnp.minimum(x, 0.0) - jnp.log1p(jnp.exp(-jnp.abs(x)))


def _rms_rows(x, w):
    return x * lax.rsqrt(jnp.mean(x * x, axis=-1, keepdims=True) + EPS) * w


def _lb_kernel(logits_ref, out_ref):
    lg = logits_ref[...]
    e = jnp.exp(lg - jnp.max(lg, axis=0, keepdims=True))
    p = e / jnp.sum(e, axis=0, keepdims=True)
    cum = p[0:1]
    first = cum
    pad = jnp.zeros((SUBLANES - 3, lg.shape[1]), F32)
    for layer in range(DEPTH):
        if layer > 0:
            cum = cum + p[layer:layer + 1]
        lb = jnp.clip(cum - first, 0.0, 1.0)
        out_ref[layer] = jnp.concatenate([jnp.log(lb), jnp.log1p(-lb), 1.0 - lb, pad], axis=0)


def _lb_params(logits):
    return pl.pallas_call(
        _lb_kernel,
        out_shape=jax.ShapeDtypeStruct((DEPTH, SUBLANES, logits.shape[1]), F32),
        name="hgrn_lower_bounds",
    )(logits.astype(F32))


PROJ_TM = 512
PROJ_TN = 1024


def _proj_kernel(x_ref, lnw_ref, w_ref, ws_ref, o_ref, os_ref, h_ref):
    @pl.when(pl.program_id(1) == 0)
    def _():
        h = _rms_rows(x_ref[...], lnw_ref[...])
        hh, hl = _split2(h)
        h_ref[...] = hh
        wh, wl = _split2(ws_ref[...])
        os_ref[...] = _dot(hh, wh) + _dot(hh, wl) + _dot(hl, wh)

    o_ref[...] = _dot(h_ref[...], w_ref[...])


def _proj(x2, lnw, w_main, w_small):
    n = x2.shape[0]
    tm = min(PROJ_TM, n)
    return pl.pallas_call(
        _proj_kernel,
        grid=(n // tm, N_MAIN // PROJ_TN),
        in_specs=[
            pl.BlockSpec((tm, D_MODEL), lambda i, j: (i, 0)),
            pl.BlockSpec((1, D_MODEL), lambda i, j: (0, 0)),
            pl.BlockSpec((D_MODEL, PROJ_TN), lambda i, j: (0, j)),
            pl.BlockSpec((D_MODEL, LANES), lambda i, j: (0, 0)),
        ],
        out_specs=[
            pl.BlockSpec((tm, PROJ_TN), lambda i, j: (i, j)),
            pl.BlockSpec((tm, LANES), lambda i, j: (i, 0)),
        ],
        out_shape=[
            jax.ShapeDtypeStruct((n, N_MAIN), F32),
            jax.ShapeDtypeStruct((n, LANES), F32),
        ],
        scratch_shapes=[pltpu.VMEM((tm, D_MODEL), BF16)],
        compiler_params=pltpu.CompilerParams(
            dimension_semantics=("arbitrary", "arbitrary"), vmem_limit_bytes=VMEM_LIMIT),
        name="rmsnorm_in_proj",
    )(x2, lnw, w_main, w_small)


def _gla_chunk(q, k, v, g, st_ref, kb_ref, heads, dk, dv):
    b = _cumsum_rows(g)
    b_last = b[CHUNK - 1:CHUNK]
    qd = (q * jnp.exp(b)).astype(BF16)
    kd = (k * jnp.exp(b_last - b)).astype(BF16)
    s_decay = jnp.exp(b_last)
    vb = v.astype(BF16)

    kb_ref[0] = k
    kb_ref[1] = b

    q_off, k_off = [], []
    for n in range(1, N_SUB):
        lo = n * SUB
        b_start = b[lo:lo + 1] - g[lo:lo + 1]
        q_off.append((q[lo:lo + SUB] * jnp.exp(b[lo:lo + SUB] - b_start)).astype(BF16))
        k_off.append((k * jnp.exp(jnp.minimum(b_start - b, 0.0))).astype(BF16))

    lane = lax.broadcasted_iota(jnp.int32, (SUBLANES, CHUNK), 1)
    n_piece = CHUNK // SUBLANES
    a_diag = [[jnp.zeros((SUBLANES, CHUNK), F32) for _ in range(n_piece)] for _ in range(heads)]
    for r in range(CHUNK):
        n, j = divmod(r, SUB)
        first_piece = (n * SUB + (SUBLANES if j >= SUBLANES else 0)) // SUBLANES
        last_piece = (n * SUB + SUB) // SUBLANES
        lo, hi = first_piece * SUBLANES, last_piece * SUBLANES
        k_r = kb_ref[0, pl.ds(r, 1), :]
        b_r = kb_ref[1, pl.ds(r, 1), :]
        p = q[lo:hi] * (k_r * jnp.exp(jnp.minimum(b[lo:hi] - b_r, 0.0)))
        for h in range(heads):
            col = jnp.sum(p[:, h * dk:(h + 1) * dk], axis=-1, keepdims=True)
            for m in range(first_piece, last_piece):
                c = col[(m - first_piece) * SUBLANES:(m - first_piece + 1) * SUBLANES]
                a_diag[h][m] = jnp.where(lane == r, c, a_diag[h][m])

    row = lax.broadcasted_iota(jnp.int32, (SUB, CHUNK), 0)
    colid = lax.broadcasted_iota(jnp.int32, (SUB, CHUNK), 1)
    outs = []
    for h in range(heads):
        ks, vs = slice(h * dk, (h + 1) * dk), slice(h * dv, (h + 1) * dv)
        rows = []
        for n in range(N_SUB):
            diag = jnp.concatenate(a_diag[h][2 * n:2 * n + 2], axis=0)
            a_n = jnp.where(colid <= row + n * SUB, diag, 0.0)
            if n > 0:
                off = _dot_nt(q_off[n - 1][:, ks], k_off[n - 1][:, ks])
                a_n = jnp.where(colid < n * SUB, off, a_n)
            rows.append(a_n)
        a = jnp.concatenate(rows, axis=0).astype(BF16)
        st = st_ref[h]
        o = _dot_nt(qd[:, ks], st.astype(BF16)) + _dot(a, vb[:, vs])
        st_ref[h] = st * s_decay[:, ks] + _dot_tn(vb[:, vs], kd[:, ks])
        outs.append(o)
    return outs


def _head_out(outs, onorm, gate, merge):
    y = jnp.concatenate([_rms_rows(o, onorm) for o in outs], axis=-1)
    return y * _silu(gate) * _sigmoid(merge)


def _mixer_a_kernel(aq_ref, af_ref, ai_ref, ag_ref, m_ref, lbp_ref, onorm_ref, y_ref, st_ref, kb_ref):
    @pl.when(pl.program_id(1) == 0)
    def _():
        st_ref[...] = jnp.zeros_like(st_ref)

    z = af_ref[...]
    log_lb, log_1m_lb, one_m_lb = lbp_ref[0:1, :], lbp_ref[1:2, :], lbp_ref[2:3, :]
    t = log_1m_lb + _log_sigmoid(z)
    g = jnp.maximum(log_lb, t) + jnp.log1p(jnp.exp(-jnp.abs(log_lb - t)))
    k = one_m_lb * _sigmoid(-z)
    q = _silu(aq_ref[...])
    outs = _gla_chunk(q, k, ai_ref[...], g, st_ref, kb_ref, A_HEADS, A_DK, A_DV)
    y_ref[...] = _head_out(outs, onorm_ref[...], ag_ref[...], m_ref[...])


def _mixer_b_kernel(qk_ref, v_ref, bg_ref, m_ref, sm_ref, wgk_ref, bgk_ref, onorm_ref, y_ref, st_ref, kb_ref):
    @pl.when(pl.program_id(1) == 0)
    def _():
        st_ref[...] = jnp.zeros_like(st_ref)

    qk = qk_ref[...]
    q = qk[:, :B_KEY] * (B_DK ** -0.5)
    k = qk[:, B_KEY:]
    gk = _dot3(sm_ref[...], wgk_ref[...]) + bgk_ref[...]
    g = _log_sigmoid(gk) / B_GATE_NORM
    outs = _gla_chunk(q, k, v_ref[...], g, st_ref, kb_ref, B_HEADS, B_DK, B_DV)
    y_ref[...] = _head_out(outs, onorm_ref[...], bg_ref[...], m_ref[...])


def _tok_index(b, t, blk):
    return (b, t, blk)


def _full_spec(shape):
    return pl.BlockSpec(shape, lambda b, t: (0,) * len(shape))


def _mixer_call(kernel, name, proj3, tok_blocks, extra_inputs, extra_specs, scratch, tb=CHUNK):
    bsz, t_len, _ = proj3.shape
    assert t_len % tb == 0 and tb % CHUNK == 0
    return pl.pallas_call(
        kernel,
        grid=(bsz, t_len // tb),
        in_specs=[pl.BlockSpec((None, tb, D_MODEL), functools.partial(_tok_index, blk=blk))
                  for blk in tok_blocks] + extra_specs,
        out_specs=pl.BlockSpec((None, tb, D_MODEL), lambda b, t: (b, t, 0)),
        out_shape=jax.ShapeDtypeStruct((bsz, t_len, D_MODEL), F32),
        scratch_shapes=scratch,
        compiler_params=pltpu.CompilerParams(
            dimension_semantics=("arbitrary", "arbitrary"), vmem_limit_bytes=VMEM_LIMIT),
        name=name,
    )(*([proj3] * len(tok_blocks)), *extra_inputs)


def _mixer_a(proj3, lbp, onorm):
    return _mixer_call(
        _mixer_a_kernel, "hgrn2_mixer", proj3,
        [BLK_AQ, BLK_AF, BLK_AI, BLK_AG, BLK_M0],
        [lbp, onorm],
        [_full_spec((SUBLANES, D_MODEL)), _full_spec((1, A_DV))],
        [pltpu.VMEM((A_HEADS, A_DV, A_DK), F32), pltpu.VMEM((2, CHUNK, A_HEADS * A_DK), F32)])


def _mixer_b(proj3, small3, wgk, bgk, onorm):
    return _mixer_call(
        _mixer_b_kernel, "gla_mixer", proj3,
        [BLK_BQK, BLK_BV, BLK_BG, BLK_M1],
        [small3, wgk, bgk, onorm],
        [pl.BlockSpec((None, CHUNK, LANES), lambda b, t: (b, t, 0)),
         _full_spec((LANES, B_KEY)), _full_spec((1, B_KEY)), _full_spec((1, B_DV))],
        [pltpu.VMEM((B_HEADS, B_DV, B_DK), F32), pltpu.VMEM((2, CHUNK, B_KEY), F32)])


def _l2norm_rows(x):
    return x * lax.rsqrt(jnp.sum(x * x, axis=-1, keepdims=True) + EPS)


def _mixer_c_kernel(cq_ref, ck_ref, cv_ref, cg_ref, m_ref, sm_ref, conv_ref, hp_ref, onorm_ref,
                    y_ref, st_ref, xb_ref):
    hist = SUBLANES

    @pl.when(pl.program_id(1) == 0)
    def _():
        st_ref[...] = jnp.zeros_like(st_ref)
        xb_ref[0:hist, :] = jnp.zeros((hist, C_QKV), F32)

    tb = cq_ref.shape[0]
    xb_ref[hist:hist + tb, 0:C_KEY] = cq_ref[...]
    xb_ref[hist:hist + tb, C_KEY:2 * C_KEY] = ck_ref[...]
    xb_ref[hist:hist + tb, 2 * C_KEY:C_QKV] = cv_ref[...]

    row = lax.broadcasted_iota(jnp.int32, (CHUNK, CHUNK), 0)
    col = lax.broadcasted_iota(jnp.int32, (CHUNK, CHUNK), 1)
    eye = jnp.where(row == col, 1.0, 0.0).astype(F32)
    causal = row >= col
    merge_mask = []
    for lvl in range(6):
        bi, bj = row >> lvl, col >> lvl
        merge_mask.append((bi == bj + 1) & ((bi & 1) == 1))

    chains = []
    shared = []
    for c in range(tb // CHUNK):
        r0 = hist + c * CHUNK
        acc = xb_ref[r0:r0 + CHUNK, :] * conv_ref[C_CONV - 1:C_CONV, :]
        for j in range(C_CONV - 1):
            shift = C_CONV - 1 - j
            acc = acc + xb_ref[r0 - shift:r0 - shift + CHUNK, :] * conv_ref[j:j + 1, :]
        qkv = _silu(acc)
        sm = sm_ref[c * CHUNK:(c + 1) * CHUNK, :]
        log_a = -jnp.exp(hp_ref[0:1, :]) * jax.nn.softplus(sm + hp_ref[1:2, :])
        beta_all = _sigmoid(sm)
        b_all = _cumsum_rows(log_a)
        b_all_t = b_all.T
        e_all = jnp.exp(b_all)
        d_all = jnp.exp(b_all[CHUNK - 1:CHUNK] - b_all)
        shared.append(e_all[CHUNK - 1:CHUNK])
        for h in range(C_HEADS):
            chains.append(dict(
                q=_l2norm_rows(qkv[:, h * C_DK:(h + 1) * C_DK]) * (C_DK ** -0.5),
                k=_l2norm_rows(qkv[:, C_KEY + h * C_DK:C_KEY + (h + 1) * C_DK]),
                v=qkv[:, 2 * C_KEY + h * C_DV:2 * C_KEY + (h + 1) * C_DV],
                b_col=b_all[:, SM_A + h:SM_A + h + 1],
                b_row=b_all_t[SM_A + h:SM_A + h + 1, :],
                beta=beta_all[:, SM_B + h:SM_B + h + 1],
                e_b=e_all[:, SM_A + h:SM_A + h + 1],
                d_b=d_all[:, SM_A + h:SM_A + h + 1]))
    xb_ref[0:hist, :] = xb_ref[tb:tb + hist, :]

    for ch in chains:
        ch["decay"] = jnp.where(causal, jnp.exp(jnp.minimum(ch["b_col"] - ch["b_row"], 0.0)), 0.0)
        ch["kb"] = ch["k"].astype(BF16)
        ch["qb"] = ch["q"].astype(BF16)
    for ch in chains:
        ch["kk"] = _dot_nt(ch["kb"], ch["kb"])
        ch["qk"] = _dot_nt(ch["qb"], ch["kb"])
    for ch in chains:
        ch["lower"] = ch["beta"] * ch["kk"] * ch["decay"]
        ch["qkd"] = (ch["qk"] * ch["decay"]).astype(BF16)
        ch["x"] = eye - jnp.where(merge_mask[0], ch["lower"], 0.0)
    for lvl in range(1, 6):
        for ch in chains:
            cpart = jnp.where(merge_mask[lvl], ch["lower"], 0.0).astype(BF16)
            ch["xb"] = ch["x"].astype(BF16)
            ch["y"] = _dot(cpart, ch["xb"]).astype(BF16)
        for ch in chains:
            ch["x"] = ch["x"] - _dot(ch["xb"], ch["y"])
    for ch in chains:
        rhs = jnp.concatenate([ch["v"] * ch["beta"], ch["k"] * (ch["beta"] * ch["e_b"])], axis=-1)
        ch["sol"] = _dot(ch["x"].astype(BF16), rhs.astype(BF16))
        ch["qe"] = (ch["q"] * ch["e_b"]).astype(BF16)
        ch["kdec"] = (ch["k"] * ch["d_b"]).astype(BF16)

    for c in range(tb // CHUNK):
        mine = chains[c * C_HEADS:(c + 1) * C_HEADS]
        sts = [st_ref[h] for h in range(C_HEADS)]
        stb = [s.astype(BF16) for s in sts]
        v_new = [ch["sol"][:, :C_DV] - _dot_nt(ch["sol"][:, C_DV:].astype(BF16), sb)
                 for ch, sb in zip(mine, stb)]
        vnb = [vn.astype(BF16) for vn in v_new]
        outs = [_dot_nt(ch["qe"], sb) + _dot(ch["qkd"], vn) for ch, sb, vn in zip(mine, stb, vnb)]
        for h in range(C_HEADS):
            a_last = shared[c][:, SM_A + h:SM_A + h + 1]
            st_ref[h] = a_last * sts[h] + _dot_tn(vnb[h], mine[h]["kdec"])
        rs = slice(c * CHUNK, (c + 1) * CHUNK)
        y_ref[rs, :] = _head_out(outs, onorm_ref[...], cg_ref[rs, :], m_ref[rs, :])


GDN_TB = 128


def _mixer_c(proj3, small3, conv, head_params, onorm):
    tb = min(GDN_TB, proj3.shape[1])
    return _mixer_call(
        _mixer_c_kernel, "gated_deltanet_mixer", proj3,
        [BLK_CQ, BLK_CK, BLK_CV, BLK_CG, BLK_M2],
        [small3, conv, head_params, onorm],
        [pl.BlockSpec((None, tb, LANES), lambda b, t: (b, t, 0)),
         _full_spec((C_CONV, C_QKV)), _full_spec((SUBLANES, LANES)), _full_spec((1, C_DV))],
        [pltpu.VMEM((C_HEADS, C_DV, C_DK), F32), pltpu.VMEM((SUBLANES + tb, C_QKV), F32)],
        tb=tb)


MLP_TM = 256


def _out_mlp_kernel(x_ref, ya_ref, yb_ref, yc_ref, wo_ref, wu_ref, wd_ref, ln_ref, o_ref):
    y = (ya_ref[...] + yb_ref[...] + yc_ref[...]).astype(BF16)
    mix = _dot(y, wo_ref[...])
    x1 = x_ref[...] + _rms_rows(mix, ln_ref[0:1, :])
    h = _rms_rows(x1, ln_ref[1:2, :]).astype(BF16)
    up = _dot(h, wu_ref[...])
    act = jnp.square(jnp.maximum(up, 0.0)).astype(BF16)
    down = _dot(act, wd_ref[...])
    o_ref[...] = x1 + _rms_rows(down, ln_ref[2:3, :])


def _out_mlp(x2, ya, yb, yc, wo, wu, wd, ln):
    n = x2.shape[0]
    tm = min(MLP_TM, n)
    tok = pl.BlockSpec((tm, D_MODEL), lambda i: (i, 0))

    def resident(shape):
        return pl.BlockSpec(shape, lambda i: (0, 0), pipeline_mode=pl.Buffered(1))

    return pl.pallas_call(
        _out_mlp_kernel,
        grid=(n // tm,),
        in_specs=[tok, tok, tok, tok,
                  resident((D_MODEL, D_MODEL)), resident((D_MODEL, D_FF)), resident((D_FF, D_MODEL)),
                  resident((SUBLANES, D_MODEL))],
        out_specs=tok,
        out_shape=jax.ShapeDtypeStruct((n, D_MODEL), F32),
        compiler_params=pltpu.CompilerParams(
            dimension_semantics=("arbitrary",), vmem_limit_bytes=VMEM_LIMIT),
        name="merge_out_proj_mlp",
    )(x2, ya, yb, yc, wo, wu, wd, ln)


def _reorder_w_in(w_in):
    o_bgk = 4 * D_MODEL + 2 * B_KEY + D_MODEL
    o_bg = o_bgk + B_RANK
    o_cqkv = o_bg + D_MODEL
    o_ca = o_cqkv + C_QKV
    o_cg = o_ca + 2 * C_HEADS
    main = jnp.concatenate([w_in[..., :o_bgk], w_in[..., o_bg:o_ca], w_in[..., o_cg:]], axis=-1)
    pad = jnp.zeros(w_in.shape[:-1] + (LANES - B_RANK - 2 * C_HEADS,), w_in.dtype)
    small = jnp.concatenate([w_in[..., o_bgk:o_bg], w_in[..., o_ca:o_cg], pad], axis=-1)
    return main.astype(BF16), small.astype(F32)


def kernel(x, ln_mix_pre, ln_mix_post, ln_mlp_pre, ln_mlp_post, w_in, hgrn_lb_logits, gla_w_gk, gla_b_gk,
           gdn_conv, gdn_a_log, gdn_dt_bias, hgrn_onorm, gla_onorm, gdn_onorm, w_out, w_up, w_down):
    bsz, t_len, _ = x.shape
    n = bsz * t_len
    assert t_len % CHUNK == 0 and n % min(PROJ_TM, n) == 0 and n % min(MLP_TM, n) == 0

    lbp = _lb_params(hgrn_lb_logits)
    w_main, w_small = _reorder_w_in(w_in)
    wgk = jnp.pad(gla_w_gk.astype(F32), ((0, 0), (0, LANES - B_RANK), (0, 0)))
    lane_pad = ((0, 0), (SM_A, LANES - SM_A - C_HEADS))
    head_params = jnp.stack([jnp.pad(gdn_a_log.astype(F32), lane_pad),
                             jnp.pad(gdn_dt_bias.astype(F32), lane_pad)], axis=1)
    head_params = jnp.pad(head_params, ((0, 0), (0, SUBLANES - 2), (0, 0)))
    ln_rest = jnp.stack([ln_mix_post, ln_mlp_pre, ln_mlp_post], axis=1).astype(F32)
    ln_rest = jnp.pad(ln_rest, ((0, 0), (0, SUBLANES - 3), (0, 0)))
    wo, wu, wd = w_out.astype(BF16), w_up.astype(BF16), w_down.astype(BF16)

    x2 = x.reshape(n, D_MODEL).astype(F32)
    for layer in range(DEPTH):
        proj, small = _proj(x2, ln_mix_pre[layer][None, :].astype(F32), w_main[layer], w_small[layer])
        proj3 = proj.reshape(bsz, t_len, N_MAIN)
        small3 = small.reshape(bsz, t_len, LANES)
        ya = _mixer_a(proj3, lbp[layer], hgrn_onorm[layer][None, :].astype(F32))
        yb = _mixer_b(proj3, small3, wgk[layer], gla_b_gk[layer][None, :].astype(F32),
                      gla_onorm[layer][None, :].astype(F32))
        yc = _mixer_c(proj3, small3, gdn_conv[layer].astype(F32), head_params[layer],
                      gdn_onorm[layer][None, :].astype(F32))
        x2 = _out_mlp(x2, ya.reshape(n, D_MODEL), yb.reshape(n, D_MODEL), yc.reshape(n, D_MODEL),
                      wo[layer], wu[layer], wd[layer], ln_rest[layer])
    return x2.reshape(bsz, t_len, D_MODEL).astype(x.dtype)
```

```python
import functools

import jax
import jax.numpy as jnp
from jax import lax
from jax.experimental import pallas as pl
from jax.experimental.pallas import tpu as pltpu

F32 = jnp.float32
BF16 = jnp.bfloat16

D_MODEL = 1024
DEPTH = 4
CHUNK = 64
SUB = 16
N_SUB = CHUNK // SUB
EPS = 1e-6
LANES = 128
SUBLANES = 8

A_HEADS, A_DK, A_DV = 8, 128, 128
B_HEADS, B_DK, B_DV = 4, 128, 256
B_KEY = B_HEADS * B_DK
B_RANK = 16
B_GATE_NORM = 16.0
C_HEADS, C_DK, C_DV = 8, 128, 128
C_KEY = C_HEADS * C_DK
C_CONV = 4
C_QKV = 2 * C_KEY + D_MODEL
D_FF = 4 * D_MODEL

BLK_AQ, BLK_AI, BLK_AG = 0, 1, 2
BLK_BQK, BLK_BV, BLK_BG = 3, 4, 5
BLK_CQ, BLK_CK, BLK_CV, BLK_CG = 6, 7, 8, 9
BLK_M0, BLK_M1, BLK_M2 = 10, 11, 12
N_MAIN = 13 * D_MODEL
SM_GK, SM_A, SM_B = 0, B_RANK, B_RANK + C_HEADS

VMEM_LIMIT = 56 * 1024 * 1024


def _dot(a, b):
    return jnp.dot(a, b, preferred_element_type=F32)


def _dot_nt(a, b):
    return lax.dot_general(a, b, (((1,), (1,)), ((), ())), preferred_element_type=F32)


def _dot_tn(a, b):
    return lax.dot_general(a, b, (((0,), (0,)), ((), ())), preferred_element_type=F32)


def _split2(a):
    hi = a.astype(BF16)
    lo = (a - hi.astype(F32)).astype(BF16)
    return hi, lo


def _dot3(a, b):
    ah, al = _split2(a)
    bh, bl = _split2(b)
    return _dot(ah, bh) + _dot(ah, bl) + _dot(al, bh)


def _cumsum_rows(g):
    row = lax.broadcasted_iota(jnp.int32, (CHUNK, CHUNK), 0)
    col = lax.broadcasted_iota(jnp.int32, (CHUNK, CHUNK), 1)
    tril = jnp.where(row >= col, 1.0, 0.0).astype(BF16)
    g1 = g.astype(BF16)
    r1 = g - g1.astype(F32)
    g2 = r1.astype(BF16)
    g3 = (r1 - g2.astype(F32)).astype(BF16)
    return _dot(tril, g1) + _dot(tril, g2) + _dot(tril, g3)


def _sigmoid(x):
    return jax.nn.sigmoid(x)


def _silu(x):
    return x * jax.nn.sigmoid(x)


def _log_sigmoid(x):
    return jnp.minimum(x, 0.0) - jnp.log1p(jnp.exp(-jnp.abs(x)))


def _rms_rows(x, w):
    return x * lax.rsqrt(jnp.mean(x * x, axis=-1, keepdims=True) + EPS) * w


def _lb_kernel(logits_ref, out_ref):
    lg = logits_ref[...]
    e = jnp.exp(lg - jnp.max(lg, axis=0, keepdims=True))
    p = e / jnp.sum(e, axis=0, keepdims=True)
    cum = p[0:1]
    first = cum
    pad = jnp.zeros((SUBLANES - 3, lg.shape[1]), F32)
    for layer in range(DEPTH):
        if layer > 0:
            cum = cum + p[layer:layer + 1]
        lb = jnp.clip(cum - first, 0.0, 1.0)
        out_ref[layer] = jnp.concatenate([jnp.log(lb), jnp.log1p(-lb), 1.0 - lb, pad], axis=0)


def _lb_params(logits):
    return pl.pallas_call(
        _lb_kernel,
        out_shape=jax.ShapeDtypeStruct((DEPTH, SUBLANES, logits.shape[1]), F32),
        name="hgrn_lower_bounds",
    )(logits.astype(F32))


PROJ_TM = 1024
PROJ_TN = 1024


def _proj_kernel(x_ref, lnw_ref, w_ref, wf_ref, ws_ref, o_ref, of_ref, os_ref, h_ref):
    @pl.when(pl.program_id(1) == 0)
    def _():
        h = _rms_rows(x_ref[...], lnw_ref[...])
        hh, hl = _split2(h)
        h_ref[...] = hh
        wh, wl = _split2(ws_ref[...])
        os_ref[...] = _dot(hh, wh) + _dot(hh, wl) + _dot(hl, wh)
        of_ref[...] = _dot(hh, wf_ref[...])

    o_ref[...] = _dot(h_ref[...], w_ref[...]).astype(BF16)


def _proj(x2, lnw, w_main, w_forget, w_small):
    n = x2.shape[0]
    tm = min(PROJ_TM, n)

    def resident(shape):
        return pl.BlockSpec(shape, lambda i, j: (0, 0), pipeline_mode=pl.Buffered(1))

    return pl.pallas_call(
        _proj_kernel,
        grid=(n // tm, N_MAIN // PROJ_TN),
        in_specs=[
            pl.BlockSpec((tm, D_MODEL), lambda i, j: (i, 0)),
            resident((1, D_MODEL)),
            pl.BlockSpec((D_MODEL, PROJ_TN), lambda i, j: (0, j)),
            resident((D_MODEL, D_MODEL)),
            resident((D_MODEL, LANES)),
        ],
        out_specs=[
            pl.BlockSpec((tm, PROJ_TN), lambda i, j: (i, j)),
            pl.BlockSpec((tm, D_MODEL), lambda i, j: (i, 0)),
            pl.BlockSpec((tm, LANES), lambda i, j: (i, 0)),
        ],
        out_shape=[
            jax.ShapeDtypeStruct((n, N_MAIN), BF16),
            jax.ShapeDtypeStruct((n, D_MODEL), F32),
            jax.ShapeDtypeStruct((n, LANES), F32),
        ],
        scratch_shapes=[pltpu.VMEM((tm, D_MODEL), BF16)],
        compiler_params=pltpu.CompilerParams(
            dimension_semantics=("arbitrary", "arbitrary"), vmem_limit_bytes=VMEM_LIMIT),
        name="rmsnorm_in_proj",
    )(x2, lnw, w_main, w_forget, w_small)


def _gla_chunk(q, k, v, g, st_ref, kb_ref, heads, dk, dv):
    b = _cumsum_rows(g)
    b_last = b[CHUNK - 1:CHUNK]
    qd = (q * jnp.exp(b)).astype(BF16)
    kd = (k * jnp.exp(b_last - b)).astype(BF16)
    s_decay = jnp.exp(b_last)
    vb = v.astype(BF16)

    kb_ref[0] = k
    kb_ref[1] = b

    q_off, k_off = [], []
    for n in range(1, N_SUB):
        lo = n * SUB
        b_start = b[lo:lo + 1] - g[lo:lo + 1]
        q_off.append((q[lo:lo + SUB] * jnp.exp(b[lo:lo + SUB] - b_start)).astype(BF16))
        k_off.append((k * jnp.exp(jnp.minimum(b_start - b, 0.0))).astype(BF16))

    lane = lax.broadcasted_iota(jnp.int32, (SUBLANES, CHUNK), 1)
    n_piece = CHUNK // SUBLANES
    a_diag = [[jnp.zeros((SUBLANES, CHUNK), F32) for _ in range(n_piece)] for _ in range(heads)]
    for r in range(CHUNK):
        n, j = divmod(r, SUB)
        first_piece = (n * SUB + (SUBLANES if j >= SUBLANES else 0)) // SUBLANES
        last_piece = (n * SUB + SUB) // SUBLANES
        lo, hi = first_piece * SUBLANES, last_piece * SUBLANES
        k_r = kb_ref[0, pl.ds(r, 1), :]
        b_r = kb_ref[1, pl.ds(r, 1), :]
        p = q[lo:hi] * (k_r * jnp.exp(jnp.minimum(b[lo:hi] - b_r, 0.0)))
        for h in range(heads):
            col = jnp.sum(p[:, h * dk:(h + 1) * dk], axis=-1, keepdims=True)
            for m in range(first_piece, last_piece):
                c = col[(m - first_piece) * SUBLANES:(m - first_piece + 1) * SUBLANES]
                a_diag[h][m] = jnp.where(lane == r, c, a_diag[h][m])

    row = lax.broadcasted_iota(jnp.int32, (SUB, CHUNK), 0)
    colid = lax.broadcasted_iota(jnp.int32, (SUB, CHUNK), 1)
    outs = []
    for h in range(heads):
        ks, vs = slice(h * dk, (h + 1) * dk), slice(h * dv, (h + 1) * dv)
        rows = []
        for n in range(N_SUB):
            diag = jnp.concatenate(a_diag[h][2 * n:2 * n + 2], axis=0)
            a_n = jnp.where(colid <= row + n * SUB, diag, 0.0)
            if n > 0:
                off = _dot_nt(q_off[n - 1][:, ks], k_off[n - 1][:, ks])
                a_n = jnp.where(colid < n * SUB, off, a_n)
            rows.append(a_n)
        a = jnp.concatenate(rows, axis=0).astype(BF16)
        st = st_ref[h]
        o = _dot_nt(qd[:, ks], st.astype(BF16)) + _dot(a, vb[:, vs])
        st_ref[h] = st * s_decay[:, ks] + _dot_tn(vb[:, vs], kd[:, ks])
        outs.append(o)
    return outs


def _head_out(outs, onorm, gate, merge):
    y = jnp.concatenate([_rms_rows(o, onorm) for o in outs], axis=-1)
    return y * _silu(gate) * _sigmoid(merge)


def _f32(ref, rows=slice(None)):
    return ref[rows, :].astype(F32)


def _mixer_a_kernel(aq_ref, ai_ref, ag_ref, m_ref, af_ref, lbp_ref, onorm_ref, y_ref, st_ref, kb_ref):
    @pl.when(pl.program_id(1) == 0)
    def _():
        st_ref[...] = jnp.zeros_like(st_ref)

    z = af_ref[...]
    log_lb, log_1m_lb, one_m_lb = lbp_ref[0:1, :], lbp_ref[1:2, :], lbp_ref[2:3, :]
    t = log_1m_lb + _log_sigmoid(z)
    g = jnp.maximum(log_lb, t) + jnp.log1p(jnp.exp(-jnp.abs(log_lb - t)))
    k = one_m_lb * _sigmoid(-z)
    q = _silu(_f32(aq_ref))
    outs = _gla_chunk(q, k, _f32(ai_ref), g, st_ref, kb_ref, A_HEADS, A_DK, A_DV)
    y_ref[...] = _head_out(outs, onorm_ref[...], _f32(ag_ref), _f32(m_ref)).astype(BF16)


def _mixer_b_kernel(qk_ref, v_ref, bg_ref, m_ref, sm_ref, wgk_ref, bgk_ref, onorm_ref, y_ref, st_ref, kb_ref):
    @pl.when(pl.program_id(1) == 0)
    def _():
        st_ref[...] = jnp.zeros_like(st_ref)

    qk = _f32(qk_ref)
    q = qk[:, :B_KEY] * (B_DK ** -0.5)
    k = qk[:, B_KEY:]
    gk = _dot3(sm_ref[...], wgk_ref[...]) + bgk_ref[...]
    g = _log_sigmoid(gk) / B_GATE_NORM
    outs = _gla_chunk(q, k, _f32(v_ref), g, st_ref, kb_ref, B_HEADS, B_DK, B_DV)
    y_ref[...] = _head_out(outs, onorm_ref[...], _f32(bg_ref), _f32(m_ref)).astype(BF16)


def _tok_index(b, t, blk):
    return (b, t, blk)


def _full_spec(shape):
    return pl.BlockSpec(shape, lambda b, t: (0,) * len(shape))


def _mixer_call(kernel, name, proj3, tok_blocks, extra_inputs, extra_specs, scratch, tb=CHUNK):
    bsz, t_len, _ = proj3.shape
    assert t_len % tb == 0 and tb % CHUNK == 0
    return pl.pallas_call(
        kernel,
        grid=(bsz, t_len // tb),
        in_specs=[pl.BlockSpec((None, tb, D_MODEL), functools.partial(_tok_index, blk=blk))
                  for blk in tok_blocks] + extra_specs,
        out_specs=pl.BlockSpec((None, tb, D_MODEL), lambda b, t: (b, t, 0)),
        out_shape=jax.ShapeDtypeStruct((bsz, t_len, D_MODEL), BF16),
        scratch_shapes=scratch,
        compiler_params=pltpu.CompilerParams(
            dimension_semantics=("arbitrary", "arbitrary"), vmem_limit_bytes=VMEM_LIMIT),
        name=name,
    )(*([proj3] * len(tok_blocks)), *extra_inputs)


def _mixer_a(proj3, forget3, lbp, onorm):
    return _mixer_call(
        _mixer_a_kernel, "hgrn2_mixer", proj3,
        [BLK_AQ, BLK_AI, BLK_AG, BLK_M0],
        [forget3, lbp, onorm],
        [pl.BlockSpec((None, CHUNK, D_MODEL), lambda b, t: (b, t, 0)),
         _full_spec((SUBLANES, D_MODEL)), _full_spec((1, A_DV))],
        [pltpu.VMEM((A_HEADS, A_DV, A_DK), F32), pltpu.VMEM((2, CHUNK, A_HEADS * A_DK), F32)])


def _mixer_b(proj3, small3, wgk, bgk, onorm):
    return _mixer_call(
        _mixer_b_kernel, "gla_mixer", proj3,
        [BLK_BQK, BLK_BV, BLK_BG, BLK_M1],
        [small3, wgk, bgk, onorm],
        [pl.BlockSpec((None, CHUNK, LANES), lambda b, t: (b, t, 0)),
         _full_spec((LANES, B_KEY)), _full_spec((1, B_KEY)), _full_spec((1, B_DV))],
        [pltpu.VMEM((B_HEADS, B_DV, B_DK), F32), pltpu.VMEM((2, CHUNK, B_KEY), F32)])


def _l2norm_rows(x):
    return x * lax.rsqrt(jnp.sum(x * x, axis=-1, keepdims=True) + EPS)


def _mixer_c_kernel(cq_ref, ck_ref, cv_ref, cg_ref, m_ref, sm_ref, conv_ref, hp_ref, onorm_ref,
                    y_ref, st_ref, xb_ref):
    hist = SUBLANES

    @pl.when(pl.program_id(1) == 0)
    def _():
        st_ref[...] = jnp.zeros_like(st_ref)
        xb_ref[0:hist, :] = jnp.zeros((hist, C_QKV), F32)

    tb = cq_ref.shape[0]
    xb_ref[hist:hist + tb, 0:C_KEY] = _f32(cq_ref)
    xb_ref[hist:hist + tb, C_KEY:2 * C_KEY] = _f32(ck_ref)
    xb_ref[hist:hist + tb, 2 * C_KEY:C_QKV] = _f32(cv_ref)

    row = lax.broadcasted_iota(jnp.int32, (CHUNK, CHUNK), 0)
    col = lax.broadcasted_iota(jnp.int32, (CHUNK, CHUNK), 1)
    eye = jnp.where(row == col, 1.0, 0.0).astype(F32)
    causal = row >= col
    merge_mask = []
    for lvl in range(6):
        bi, bj = row >> lvl, col >> lvl
        merge_mask.append((bi == bj + 1) & ((bi & 1) == 1))

    chains = []
    shared = []
    for c in range(tb // CHUNK):
        r0 = hist + c * CHUNK
        acc = xb_ref[r0:r0 + CHUNK, :] * conv_ref[C_CONV - 1:C_CONV, :]
        for j in range(C_CONV - 1):
            shift = C_CONV - 1 - j
            acc = acc + xb_ref[r0 - shift:r0 - shift + CHUNK, :] * conv_ref[j:j + 1, :]
        qkv = _silu(acc)
        sm = sm_ref[c * CHUNK:(c + 1) * CHUNK, :]
        log_a = -jnp.exp(hp_ref[0:1, :]) * jax.nn.softplus(sm + hp_ref[1:2, :])
        beta_all = _sigmoid(sm)
        b_all = _cumsum_rows(log_a)
        b_all_t = b_all.T
        e_all = jnp.exp(b_all)
        d_all = jnp.exp(b_all[CHUNK - 1:CHUNK] - b_all)
        shared.append(e_all[CHUNK - 1:CHUNK])
        for h in range(C_HEADS):
            chains.append(dict(
                q=_l2norm_rows(qkv[:, h * C_DK:(h + 1) * C_DK]) * (C_DK ** -0.5),
                k=_l2norm_rows(qkv[:, C_KEY + h * C_DK:C_KEY + (h + 1) * C_DK]),
                v=qkv[:, 2 * C_KEY + h * C_DV:2 * C_KEY + (h + 1) * C_DV],
                b_col=b_all[:, SM_A + h:SM_A + h + 1],
                b_row=b_all_t[SM_A + h:SM_A + h + 1, :],
                beta=beta_all[:, SM_B + h:SM_B + h + 1],
                e_b=e_all[:, SM_A + h:SM_A + h + 1],
                d_b=d_all[:, SM_A + h:SM_A + h + 1]))
    xb_ref[0:hist, :] = xb_ref[tb:tb + hist, :]

    for ch in chains:
        ch["decay"] = jnp.where(causal, jnp.exp(jnp.minimum(ch["b_col"] - ch["b_row"], 0.0)), 0.0)
        ch["kb"] = ch["k"].astype(BF16)
        ch["qb"] = ch["q"].astype(BF16)
    for ch in chains:
        ch["kk"] = _dot_nt(ch["kb"], ch["kb"])
        ch["qk"] = _dot_nt(ch["qb"], ch["kb"])
    for ch in chains:
        ch["lower"] = ch["beta"] * ch["kk"] * ch["decay"]
        ch["qkd"] = (ch["qk"] * ch["decay"]).astype(BF16)
        ch["x"] = eye - jnp.where(merge_mask[0], ch["lower"], 0.0)
    for lvl in range(1, 6):
        for ch in chains:
            cpart = jnp.where(merge_mask[lvl], ch["lower"], 0.0).astype(BF16)
            ch["xb"] = ch["x"].astype(BF16)
            ch["y"] = _dot(cpart, ch["xb"]).astype(BF16)
        for ch in chains:
            ch["x"] = ch["x"] - _dot(ch["xb"], ch["y"])
    for ch in chains:
        rhs = jnp.concatenate([ch["v"] * ch["beta"], ch["k"] * (ch["beta"] * ch["e_b"])], axis=-1)
        ch["sol"] = _dot(ch["x"].astype(BF16), rhs.astype(BF16))
        ch["qe"] = (ch["q"] * ch["e_b"]).astype(BF16)
        ch["kdec"] = (ch["k"] * ch["d_b"]).astype(BF16)

    for c in range(tb // CHUNK):
        mine = chains[c * C_HEADS:(c + 1) * C_HEADS]
        sts = [st_ref[h] for h in range(C_HEADS)]
        stb = [s.astype(BF16) for s in sts]
        v_new = [ch["sol"][:, :C_DV] - _dot_nt(ch["sol"][:, C_DV:].astype(BF16), sb)
                 for ch, sb in zip(mine, stb)]
        vnb = [vn.astype(BF16) for vn in v_new]
        outs = [_dot_nt(ch["qe"], sb) + _dot(ch["qkd"], vn) for ch, sb, vn in zip(mine, stb, vnb)]
        for h in range(C_HEADS):
            a_last = shared[c][:, SM_A + h:SM_A + h + 1]
            st_ref[h] = a_last * sts[h] + _dot_tn(vnb[h], mine[h]["kdec"])
        rs = slice(c * CHUNK, (c + 1) * CHUNK)
        y_ref[rs, :] = _head_out(outs, onorm_ref[...], _f32(cg_ref, rs), _f32(m_ref, rs)).astype(BF16)


GDN_TB = 128


def _mixer_c(proj3, small3, conv, head_params, onorm):
    tb = min(GDN_TB, proj3.shape[1])
    return _mixer_call(
        _mixer_c_kernel, "gated_deltanet_mixer", proj3,
        [BLK_CQ, BLK_CK, BLK_CV, BLK_CG, BLK_M2],
        [small3, conv, head_params, onorm],
        [pl.BlockSpec((None, tb, LANES), lambda b, t: (b, t, 0)),
         _full_spec((C_CONV, C_QKV)), _full_spec((SUBLANES, LANES)), _full_spec((1, C_DV))],
        [pltpu.VMEM((C_HEADS, C_DV, C_DK), F32), pltpu.VMEM((SUBLANES + tb, C_QKV), F32)],
        tb=tb)


MLP_TM = 256


def _out_mlp_kernel(x_ref, ya_ref, yb_ref, yc_ref, wo_ref, wu_ref, wd_ref, ln_ref, o_ref):
    y = (_f32(ya_ref) + _f32(yb_ref) + _f32(yc_ref)).astype(BF16)
    mix = _dot(y, wo_ref[...])
    x1 = x_ref[...] + _rms_rows(mix, ln_ref[0:1, :])
    h = _rms_rows(x1, ln_ref[1:2, :]).astype(BF16)
    up = _dot(h, wu_ref[...])
    act = jnp.square(jnp.maximum(up, 0.0)).astype(BF16)
    down = _dot(act, wd_ref[...])
    o_ref[...] = x1 + _rms_rows(down, ln_ref[2:3, :])


def _out_mlp(x2, ya, yb, yc, wo, wu, wd, ln):
    n = x2.shape[0]
    tm = min(MLP_TM, n)
    tok = pl.BlockSpec((tm, D_MODEL), lambda i: (i, 0))

    def resident(shape):
        return pl.BlockSpec(shape, lambda i: (0, 0), pipeline_mode=pl.Buffered(1))

    return pl.pallas_call(
        _out_mlp_kernel,
        grid=(n // tm,),
        in_specs=[tok, tok, tok, tok,
                  resident((D_MODEL, D_MODEL)), resident((D_MODEL, D_FF)), resident((D_FF, D_MODEL)),
                  resident((SUBLANES, D_MODEL))],
        out_specs=tok,
        out_shape=jax.ShapeDtypeStruct((n, D_MODEL), F32),
        compiler_params=pltpu.CompilerParams(
            dimension_semantics=("arbitrary",), vmem_limit_bytes=VMEM_LIMIT),
        name="merge_out_proj_mlp",
    )(x2, ya, yb, yc, wo, wu, wd, ln)


def _reorder_w_in(w_in):
    o_bgk = 4 * D_MODEL + 2 * B_KEY + D_MODEL
    o_bg = o_bgk + B_RANK
    o_cqkv = o_bg + D_MODEL
    o_ca = o_cqkv + C_QKV
    o_cg = o_ca + 2 * C_HEADS
    main = jnp.concatenate([w_in[..., :D_MODEL], w_in[..., 2 * D_MODEL:o_bgk], w_in[..., o_bg:o_ca],
                            w_in[..., o_cg:]], axis=-1)
    forget = w_in[..., D_MODEL:2 * D_MODEL]
    pad = jnp.zeros(w_in.shape[:-1] + (LANES - B_RANK - 2 * C_HEADS,), w_in.dtype)
    small = jnp.concatenate([w_in[..., o_bgk:o_bg], w_in[..., o_ca:o_cg], pad], axis=-1)
    return main.astype(BF16), forget.astype(BF16), small.astype(F32)


def kernel(x, ln_mix_pre, ln_mix_post, ln_mlp_pre, ln_mlp_post, w_in, hgrn_lb_logits, gla_w_gk, gla_b_gk,
           gdn_conv, gdn_a_log, gdn_dt_bias, hgrn_onorm, gla_onorm, gdn_onorm, w_out, w_up, w_down):
    bsz, t_len, _ = x.shape
    n = bsz * t_len
    assert t_len % CHUNK == 0 and n % min(PROJ_TM, n) == 0 and n % min(MLP_TM, n) == 0

    lbp = _lb_params(hgrn_lb_logits)
    w_main, w_forget, w_small = _reorder_w_in(w_in)
    wgk = jnp.pad(gla_w_gk.astype(F32), ((0, 0), (0, LANES - B_RANK), (0, 0)))
    lane_pad = ((0, 0), (SM_A, LANES - SM_A - C_HEADS))
    head_params = jnp.stack([jnp.pad(gdn_a_log.astype(F32), lane_pad),
                             jnp.pad(gdn_dt_bias.astype(F32), lane_pad)], axis=1)
    head_params = jnp.pad(head_params, ((0, 0), (0, SUBLANES - 2), (0, 0)))
    ln_rest = jnp.stack([ln_mix_post, ln_mlp_pre, ln_mlp_post], axis=1).astype(F32)
    ln_rest = jnp.pad(ln_rest, ((0, 0), (0, SUBLANES - 3), (0, 0)))
    wo, wu, wd = w_out.astype(BF16), w_up.astype(BF16), w_down.astype(BF16)

    x2 = x.reshape(n, D_MODEL).astype(F32)
    for layer in range(DEPTH):
        proj, forget, small = _proj(x2, ln_mix_pre[layer][None, :].astype(F32), w_main[layer],
                                    w_forget[layer], w_small[layer])
        proj3 = proj.reshape(bsz, t_len, N_MAIN)
        small3 = small.reshape(bsz, t_len, LANES)
        ya = _mixer_a(proj3, forget.reshape(bsz, t_len, D_MODEL), lbp[layer],
                      hgrn_onorm[layer][None, :].astype(F32))
        yb = _mixer_b(proj3, small3, wgk[layer], gla_b_gk[layer][None, :].astype(F32),
                      gla_onorm[layer][None, :].astype(F32))
        yc = _mixer_c(proj3, small3, gdn_conv[layer].astype(F32), head_params[layer],
                      gdn_onorm[layer][None, :].astype(F32))
        x2 = _out_mlp(x2, ya.reshape(n, D_MODEL), yb.reshape(n, D_MODEL), yc.reshape(n, D_MODEL),
                      wo[layer], wu[layer], wd[layer], ln_rest[layer])
    return x2.reshape(bsz, t_len, D_MODEL).astype(x.dtype)
```

```python
import functools

import jax
import jax.numpy as jnp
from jax import lax
from jax.experimental import pallas as pl
from jax.experimental.pallas import tpu as pltpu

F32 = jnp.float32
BF16 = jnp.bfloat16

D_MODEL = 1024
DEPTH = 4
CHUNK = 64
SUB = 16
N_SUB = CHUNK // SUB
EPS = 1e-6
LOG2_E = 1.4426950408889634
LANES = 128
SUBLANES = 8

A_HEADS, A_DK, A_DV = 8, 128, 128
B_HEADS, B_DK, B_DV = 4, 128, 256
B_KEY = B_HEADS * B_DK
B_RANK = 16
B_GATE_NORM = 16.0
C_HEADS, C_DK, C_DV = 8, 128, 128
C_KEY = C_HEADS * C_DK
C_CONV = 4
C_QKV = 2 * C_KEY + D_MODEL
D_FF = 4 * D_MODEL

BLK_AQ, BLK_AI, BLK_AG = 0, 1, 2
BLK_BQK, BLK_BV, BLK_BG = 3, 4, 5
BLK_CQ, BLK_CK, BLK_CV, BLK_CG = 6, 7, 8, 9
BLK_M0, BLK_M1, BLK_M2 = 10, 11, 12
N_MAIN = 13 * D_MODEL
SM_GK, SM_A, SM_B = 0, B_RANK, B_RANK + C_HEADS

VMEM_LIMIT = 56 * 1024 * 1024


def _dot(a, b):
    return jnp.dot(a, b, preferred_element_type=F32)


def _dot_nt(a, b):
    return lax.dot_general(a, b, (((1,), (1,)), ((), ())), preferred_element_type=F32)


def _dot_tn(a, b):
    return lax.dot_general(a, b, (((0,), (0,)), ((), ())), preferred_element_type=F32)


def _split2(a):
    hi = a.astype(BF16)
    lo = (a - hi.astype(F32)).astype(BF16)
    return hi, lo


def _dot3(a, b):
    ah, al = _split2(a)
    bh, bl = _split2(b)
    return _dot(ah, bh) + _dot(ah, bl) + _dot(al, bh)


def _cumsum_rows(g):
    row = lax.broadcasted_iota(jnp.int32, (CHUNK, CHUNK), 0)
    col = lax.broadcasted_iota(jnp.int32, (CHUNK, CHUNK), 1)
    tril = jnp.where(row >= col, 1.0, 0.0).astype(BF16)
    g1 = g.astype(BF16)
    r1 = g - g1.astype(F32)
    g2 = r1.astype(BF16)
    g3 = (r1 - g2.astype(F32)).astype(BF16)
    return _dot(tril, g1) + _dot(tril, g2) + _dot(tril, g3)


def _sigmoid(x):
    return jax.nn.sigmoid(x)


def _silu(x):
    return x * jax.nn.sigmoid(x)


def _log1p_exp_neg_abs(x):
    return jnp.log(1.0 + jnp.exp(-jnp.abs(x)))


def _log_sigmoid(x):
    return jnp.minimum(x, 0.0) - _log1p_exp_neg_abs(x)


def _rms_rows(x, w):
    return x * lax.rsqrt(jnp.mean(x * x, axis=-1, keepdims=True) + EPS) * w


def _lb_kernel(logits_ref, out_ref):
    lg = logits_ref[...]
    e = jnp.exp(lg - jnp.max(lg, axis=0, keepdims=True))
    p = e / jnp.sum(e, axis=0, keepdims=True)
    cum = p[0:1]
    first = cum
    pad = jnp.zeros((SUBLANES - 3, lg.shape[1]), F32)
    for layer in range(DEPTH):
        if layer > 0:
            cum = cum + p[layer:layer + 1]
        lb = jnp.clip(cum - first, 0.0, 1.0)
        out_ref[layer] = jnp.concatenate([jnp.log(lb), jnp.log1p(-lb), 1.0 - lb, pad], axis=0)


def _lb_params(logits):
    return pl.pallas_call(
        _lb_kernel,
        out_shape=jax.ShapeDtypeStruct((DEPTH, SUBLANES, logits.shape[1]), F32),
        name="hgrn_lower_bounds",
    )(logits.astype(F32))


PROJ_TM = 1024
PROJ_TN = 1024


def _proj_kernel(x_ref, lnw_ref, w_ref, wf_ref, ws_ref, o_ref, of_ref, os_ref, h_ref):
    @pl.when(pl.program_id(1) == 0)
    def _():
        h = _rms_rows(x_ref[...], lnw_ref[...])
        hh, hl = _split2(h)
        h_ref[...] = hh
        wh, wl = _split2(ws_ref[...])
        os_ref[...] = _dot(hh, wh) + _dot(hh, wl) + _dot(hl, wh)
        of_ref[...] = _dot(hh, wf_ref[...])

    o_ref[...] = _dot(h_ref[...], w_ref[...]).astype(BF16)


def _proj(x2, lnw, w_main, w_forget, w_small):
    n = x2.shape[0]
    tm = min(PROJ_TM, n)

    def resident(shape):
        return pl.BlockSpec(shape, lambda i, j: (0, 0), pipeline_mode=pl.Buffered(1))

    return pl.pallas_call(
        _proj_kernel,
        grid=(n // tm, N_MAIN // PROJ_TN),
        in_specs=[
            pl.BlockSpec((tm, D_MODEL), lambda i, j: (i, 0)),
            resident((1, D_MODEL)),
            pl.BlockSpec((D_MODEL, PROJ_TN), lambda i, j: (0, j)),
            resident((D_MODEL, D_MODEL)),
            resident((D_MODEL, LANES)),
        ],
        out_specs=[
            pl.BlockSpec((tm, PROJ_TN), lambda i, j: (i, j)),
            pl.BlockSpec((tm, D_MODEL), lambda i, j: (i, 0)),
            pl.BlockSpec((tm, LANES), lambda i, j: (i, 0)),
        ],
        out_shape=[
            jax.ShapeDtypeStruct((n, N_MAIN), BF16),
            jax.ShapeDtypeStruct((n, D_MODEL), F32),
            jax.ShapeDtypeStruct((n, LANES), F32),
        ],
        scratch_shapes=[pltpu.VMEM((tm, D_MODEL), BF16)],
        compiler_params=pltpu.CompilerParams(
            dimension_semantics=("arbitrary", "arbitrary"), vmem_limit_bytes=VMEM_LIMIT),
        name="rmsnorm_in_proj",
    )(x2, lnw, w_main, w_forget, w_small)


def _gla_chunk(q, k, v, g, st_ref, kb_ref, heads, dk, dv):
    width = heads * dk
    g = g * LOG2_E
    b = _cumsum_rows(g)
    b_last = b[CHUNK - 1:CHUNK]
    qd = (q * jnp.exp2(b)).astype(BF16)
    kd = (k * jnp.exp2(b_last - b)).astype(BF16)
    s_decay = jnp.exp2(b_last)
    vb = v.astype(BF16)
    qb = q.astype(BF16)
    kb = k.astype(BF16)

    kb_ref[0] = b - g
    kb_ref[1] = b

    subl = lax.broadcasted_iota(jnp.int32, (SUBLANES, width), 0)

    def block_rows(idx, s, off):
        cache, pieces = {}, []

        def bcast(r):
            if r not in cache:
                cache[r] = jnp.broadcast_to(kb_ref[idx, pl.ds(r, 1), :], (SUBLANES, width))
            return cache[r]

        for base in range(0, CHUNK, SUBLANES):
            if s >= SUBLANES:
                pieces.append(bcast(s * (base // s) + off))
            else:
                n_blk = SUBLANES // s
                piece = bcast(base + (n_blk - 1) * s + off)
                for t in reversed(range(n_blk - 1)):
                    piece = jnp.where(subl < (t + 1) * s, bcast(base + t * s + off), piece)
                pieces.append(piece)
        return jnp.concatenate(pieces, axis=0)

    q_lv = [(q * jnp.exp2(g)).astype(BF16)]
    k_lv = [kb]
    odd = (subl & 1) == 1
    g_prev, g_next = [], []
    for base in range(0, CHUNK, SUBLANES):
        piece = g[base:base + SUBLANES]
        g_prev.append(jnp.where(odd, pltpu.roll(piece, 1, 0), 0.0))
        g_next.append(jnp.where(odd, 0.0, pltpu.roll(piece, SUBLANES - 1, 0)))
    q_lv.append((q * jnp.exp2(g + jnp.concatenate(g_prev, axis=0))).astype(BF16))
    k_lv.append((k * jnp.exp2(jnp.concatenate(g_next, axis=0))).astype(BF16))
    for lvl in range(2, 6):
        s = 1 << lvl
        q_lv.append((q * jnp.exp2(b - block_rows(0, s, 0))).astype(BF16))
        k_lv.append((k * jnp.exp2(block_rows(1, s, s - 1) - b)).astype(BF16))

    row = lax.broadcasted_iota(jnp.int32, (CHUNK, CHUNK), 0)
    col = lax.broadcasted_iota(jnp.int32, (CHUNK, CHUNK), 1)
    eye = row == col
    sibling = []
    for lvl in range(6):
        bi, bj = row >> lvl, col >> lvl
        sibling.append((bi == bj + 1) & ((bi & 1) == 1))

    hk = [slice(h * dk, (h + 1) * dk) for h in range(heads)]
    hv = [slice(h * dv, (h + 1) * dv) for h in range(heads)]
    first = [_dot_nt(jnp.concatenate([q_lv[0][:, s], qb[:, s]], axis=0), kb[:, s]) for s in hk]
    a = [jnp.where(sibling[0], f[:CHUNK], jnp.where(eye, f[CHUNK:], 0.0)) for f in first]
    for lvl in range(1, 6):
        prods = [_dot_nt(q_lv[lvl][:, s], k_lv[lvl][:, s]) for s in hk]
        a = [jnp.where(sibling[lvl], p, a_h) for p, a_h in zip(prods, a)]
    sts = [st_ref[h] for h in range(heads)]
    outs = [_dot_nt(qd[:, hk[h]], sts[h].astype(BF16)) + _dot(a[h].astype(BF16), vb[:, hv[h]])
            for h in range(heads)]
    for h in range(heads):
        st_ref[h] = sts[h] * s_decay[:, hk[h]] + _dot_tn(vb[:, hv[h]], kd[:, hk[h]])
    return outs


def _head_out(outs, onorm, gate, merge):
    y = jnp.concatenate([_rms_rows(o, onorm) for o in outs], axis=-1)
    return y * _silu(gate) * _sigmoid(merge)


def _f32(ref, rows=slice(None)):
    return ref[rows, :].astype(F32)


def _mixer_a_kernel(aq_ref, ai_ref, ag_ref, m_ref, af_ref, lbp_ref, onorm_ref, y_ref, st_ref, kb_ref):
    @pl.when(pl.program_id(1) == 0)
    def _():
        st_ref[...] = jnp.zeros_like(st_ref)

    z = af_ref[...]
    log_lb, log_1m_lb, one_m_lb = lbp_ref[0:1, :], lbp_ref[1:2, :], lbp_ref[2:3, :]
    t = log_1m_lb + _log_sigmoid(z)
    g = jnp.maximum(log_lb, t) + _log1p_exp_neg_abs(log_lb - t)
    k = one_m_lb * _sigmoid(-z)
    q = _silu(_f32(aq_ref))
    outs = _gla_chunk(q, k, _f32(ai_ref), g, st_ref, kb_ref, A_HEADS, A_DK, A_DV)
    y_ref[...] = _head_out(outs, onorm_ref[...], _f32(ag_ref), _f32(m_ref)).astype(BF16)


def _mixer_b_kernel(qk_ref, v_ref, bg_ref, m_ref, sm_ref, wgk_ref, bgk_ref, onorm_ref, y_ref, st_ref, kb_ref):
    @pl.when(pl.program_id(1) == 0)
    def _():
        st_ref[...] = jnp.zeros_like(st_ref)

    qk = _f32(qk_ref)
    q = qk[:, :B_KEY] * (B_DK ** -0.5)
    k = qk[:, B_KEY:]
    gk = _dot3(sm_ref[...], wgk_ref[...]) + bgk_ref[...]
    g = _log_sigmoid(gk) / B_GATE_NORM
    outs = _gla_chunk(q, k, _f32(v_ref), g, st_ref, kb_ref, B_HEADS, B_DK, B_DV)
    y_ref[...] = _head_out(outs, onorm_ref[...], _f32(bg_ref), _f32(m_ref)).astype(BF16)


def _tok_index(b, t, blk):
    return (b, t, blk)


def _full_spec(shape):
    return pl.BlockSpec(shape, lambda b, t: (0,) * len(shape))


def _mixer_call(kernel, name, proj3, tok_blocks, extra_inputs, extra_specs, scratch, tb=CHUNK):
    bsz, t_len, _ = proj3.shape
    assert t_len % tb == 0 and tb % CHUNK == 0
    return pl.pallas_call(
        kernel,
        grid=(bsz, t_len // tb),
        in_specs=[pl.BlockSpec((None, tb, D_MODEL), functools.partial(_tok_index, blk=blk))
                  for blk in tok_blocks] + extra_specs,
        out_specs=pl.BlockSpec((None, tb, D_MODEL), lambda b, t: (b, t, 0)),
        out_shape=jax.ShapeDtypeStruct((bsz, t_len, D_MODEL), BF16),
        scratch_shapes=scratch,
        compiler_params=pltpu.CompilerParams(
            dimension_semantics=("arbitrary", "arbitrary"), vmem_limit_bytes=VMEM_LIMIT),
        name=name,
    )(*([proj3] * len(tok_blocks)), *extra_inputs)


def _mixer_a(proj3, forget3, lbp, onorm):
    return _mixer_call(
        _mixer_a_kernel, "hgrn2_mixer", proj3,
        [BLK_AQ, BLK_AI, BLK_AG, BLK_M0],
        [forget3, lbp, onorm],
        [pl.BlockSpec((None, CHUNK, D_MODEL), lambda b, t: (b, t, 0)),
         _full_spec((SUBLANES, D_MODEL)), _full_spec((1, A_DV))],
        [pltpu.VMEM((A_HEADS, A_DV, A_DK), F32), pltpu.VMEM((2, CHUNK, A_HEADS * A_DK), F32)])


def _mixer_b(proj3, small3, wgk, bgk, onorm):
    return _mixer_call(
        _mixer_b_kernel, "gla_mixer", proj3,
        [BLK_BQK, BLK_BV, BLK_BG, BLK_M1],
        [small3, wgk, bgk, onorm],
        [pl.BlockSpec((None, CHUNK, LANES), lambda b, t: (b, t, 0)),
         _full_spec((LANES, B_KEY)), _full_spec((1, B_KEY)), _full_spec((1, B_DV))],
        [pltpu.VMEM((B_HEADS, B_DV, B_DK), F32), pltpu.VMEM((2, CHUNK, B_KEY), F32)])


def _l2norm_rows(x):
    return x * lax.rsqrt(jnp.sum(x * x, axis=-1, keepdims=True) + EPS)


def _mixer_c_kernel(cq_ref, ck_ref, cv_ref, cg_ref, m_ref, sm_ref, conv_ref, hp_ref, onorm_ref,
                    y_ref, st_ref, xb_ref):
    hist = SUBLANES

    @pl.when(pl.program_id(1) == 0)
    def _():
        st_ref[...] = jnp.zeros_like(st_ref)
        xb_ref[0:hist, :] = jnp.zeros((hist, C_QKV), F32)

    tb = cq_ref.shape[0]
    xb_ref[hist:hist + tb, 0:C_KEY] = _f32(cq_ref)
    xb_ref[hist:hist + tb, C_KEY:2 * C_KEY] = _f32(ck_ref)
    xb_ref[hist:hist + tb, 2 * C_KEY:C_QKV] = _f32(cv_ref)

    row = lax.broadcasted_iota(jnp.int32, (CHUNK, CHUNK), 0)
    col = lax.broadcasted_iota(jnp.int32, (CHUNK, CHUNK), 1)
    eye = jnp.where(row == col, 1.0, 0.0).astype(F32)
    causal = row >= col
    merge_mask = []
    for lvl in range(6):
        bi, bj = row >> lvl, col >> lvl
        merge_mask.append((bi == bj + 1) & ((bi & 1) == 1))

    chains = []
    shared = []
    for c in range(tb // CHUNK):
        r0 = hist + c * CHUNK
        acc = xb_ref[r0:r0 + CHUNK, :] * conv_ref[C_CONV - 1:C_CONV, :]
        for j in range(C_CONV - 1):
            shift = C_CONV - 1 - j
            acc = acc + xb_ref[r0 - shift:r0 - shift + CHUNK, :] * conv_ref[j:j + 1, :]
        qkv = _silu(acc)
        sm = sm_ref[c * CHUNK:(c + 1) * CHUNK, :]
        log_a = -jnp.exp(hp_ref[0:1, :]) * jax.nn.softplus(sm + hp_ref[1:2, :])
        beta_all = _sigmoid(sm)
        b_all = _cumsum_rows(log_a)
        b_all_t = b_all.T
        e_all = jnp.exp(b_all)
        d_all = jnp.exp(b_all[CHUNK - 1:CHUNK] - b_all)
        shared.append(e_all[CHUNK - 1:CHUNK])
        for h in range(C_HEADS):
            chains.append(dict(
                q=_l2norm_rows(qkv[:, h * C_DK:(h + 1) * C_DK]) * (C_DK ** -0.5),
                k=_l2norm_rows(qkv[:, C_KEY + h * C_DK:C_KEY + (h + 1) * C_DK]),
                v=qkv[:, 2 * C_KEY + h * C_DV:2 * C_KEY + (h + 1) * C_DV],
                b_col=b_all[:, SM_A + h:SM_A + h + 1],
                b_row=b_all_t[SM_A + h:SM_A + h + 1, :],
                beta=beta_all[:, SM_B + h:SM_B + h + 1],
                e_b=e_all[:, SM_A + h:SM_A + h + 1],
                d_b=d_all[:, SM_A + h:SM_A + h + 1]))
    xb_ref[0:hist, :] = xb_ref[tb:tb + hist, :]

    for ch in chains:
        ch["decay"] = jnp.where(causal, jnp.exp(jnp.minimum(ch["b_col"] - ch["b_row"], 0.0)), 0.0)
        ch["kb"] = ch["k"].astype(BF16)
        ch["qb"] = ch["q"].astype(BF16)
    for ch in chains:
        ch["kk"] = _dot_nt(ch["kb"], ch["kb"])
        ch["qk"] = _dot_nt(ch["qb"], ch["kb"])
    for ch in chains:
        ch["lower"] = ch["beta"] * ch["kk"] * ch["decay"]
        ch["qkd"] = (ch["qk"] * ch["decay"]).astype(BF16)
        ch["x"] = eye - jnp.where(merge_mask[0], ch["lower"], 0.0)
    for lvl in range(1, 6):
        for ch in chains:
            cpart = jnp.where(merge_mask[lvl], ch["lower"], 0.0).astype(BF16)
            ch["xb"] = ch["x"].astype(BF16)
            ch["y"] = _dot(cpart, ch["xb"]).astype(BF16)
        for ch in chains:
            ch["x"] = ch["x"] - _dot(ch["xb"], ch["y"])
    for ch in chains:
        rhs = jnp.concatenate([ch["v"] * ch["beta"], ch["k"] * (ch["beta"] * ch["e_b"])], axis=-1)
        ch["sol"] = _dot(ch["x"].astype(BF16), rhs.astype(BF16))
        ch["qe"] = (ch["q"] * ch["e_b"]).astype(BF16)
        ch["kdec"] = (ch["k"] * ch["d_b"]).astype(BF16)

    for c in range(tb // CHUNK):
        mine = chains[c * C_HEADS:(c + 1) * C_HEADS]
        sts = [st_ref[h] for h in range(C_HEADS)]
        stb = [s.astype(BF16) for s in sts]
        v_new = [ch["sol"][:, :C_DV] - _dot_nt(ch["sol"][:, C_DV:].astype(BF16), sb)
                 for ch, sb in zip(mine, stb)]
        vnb = [vn.astype(BF16) for vn in v_new]
        outs = [_dot_nt(ch["qe"], sb) + _dot(ch["qkd"], vn) for ch, sb, vn in zip(mine, stb, vnb)]
        for h in range(C_HEADS):
            a_last = shared[c][:, SM_A + h:SM_A + h + 1]
            st_ref[h] = a_last * sts[h] + _dot_tn(vnb[h], mine[h]["kdec"])
        rs = slice(c * CHUNK, (c + 1) * CHUNK)
        y_ref[rs, :] = _head_out(outs, onorm_ref[...], _f32(cg_ref, rs), _f32(m_ref, rs)).astype(BF16)


GDN_TB = 128


def _mixer_c(proj3, small3, conv, head_params, onorm):
    tb = min(GDN_TB, proj3.shape[1])
    return _mixer_call(
        _mixer_c_kernel, "gated_deltanet_mixer", proj3,
        [BLK_CQ, BLK_CK, BLK_CV, BLK_CG, BLK_M2],
        [small3, conv, head_params, onorm],
        [pl.BlockSpec((None, tb, LANES), lambda b, t: (b, t, 0)),
         _full_spec((C_CONV, C_QKV)), _full_spec((SUBLANES, LANES)), _full_spec((1, C_DV))],
        [pltpu.VMEM((C_HEADS, C_DV, C_DK), F32), pltpu.VMEM((SUBLANES + tb, C_QKV), F32)],
        tb=tb)


MLP_TM = 256


def _out_mlp_kernel(x_ref, ya_ref, yb_ref, yc_ref, wo_ref, wu_ref, wd_ref, ln_ref, o_ref):
    y = (_f32(ya_ref) + _f32(yb_ref) + _f32(yc_ref)).astype(BF16)
    mix = _dot(y, wo_ref[...])
    x1 = x_ref[...] + _rms_rows(mix, ln_ref[0:1, :])
    h = _rms_rows(x1, ln_ref[1:2, :]).astype(BF16)
    up = _dot(h, wu_ref[...])
    act = jnp.square(jnp.maximum(up, 0.0)).astype(BF16)
    down = _dot(act, wd_ref[...])
    o_ref[...] = x1 + _rms_rows(down, ln_ref[2:3, :])


def _out_mlp(x2, ya, yb, yc, wo, wu, wd, ln):
    n = x2.shape[0]
    tm = min(MLP_TM, n)
    tok = pl.BlockSpec((tm, D_MODEL), lambda i: (i, 0))

    def resident(shape):
        return pl.BlockSpec(shape, lambda i: (0, 0), pipeline_mode=pl.Buffered(1))

    return pl.pallas_call(
        _out_mlp_kernel,
        grid=(n // tm,),
        in_specs=[tok, tok, tok, tok,
                  resident((D_MODEL, D_MODEL)), resident((D_MODEL, D_FF)), resident((D_FF, D_MODEL)),
                  resident((SUBLANES, D_MODEL))],
        out_specs=tok,
        out_shape=jax.ShapeDtypeStruct((n, D_MODEL), F32),
        compiler_params=pltpu.CompilerParams(
            dimension_semantics=("arbitrary",), vmem_limit_bytes=VMEM_LIMIT),
        name="merge_out_proj_mlp",
    )(x2, ya, yb, yc, wo, wu, wd, ln)


def _reorder_w_in(w_in):
    o_bgk = 4 * D_MODEL + 2 * B_KEY + D_MODEL
    o_bg = o_bgk + B_RANK
    o_cqkv = o_bg + D_MODEL
    o_ca = o_cqkv + C_QKV
    o_cg = o_ca + 2 * C_HEADS
    main = jnp.concatenate([w_in[..., :D_MODEL], w_in[..., 2 * D_MODEL:o_bgk], w_in[..., o_bg:o_ca],
                            w_in[..., o_cg:]], axis=-1)
    forget = w_in[..., D_MODEL:2 * D_MODEL]
    pad = jnp.zeros(w_in.shape[:-1] + (LANES - B_RANK - 2 * C_HEADS,), w_in.dtype)
    small = jnp.concatenate([w_in[..., o_bgk:o_bg], w_in[..., o_ca:o_cg], pad], axis=-1)
    return main.astype(BF16), forget.astype(BF16), small.astype(F32)


def kernel(x, ln_mix_pre, ln_mix_post, ln_mlp_pre, ln_mlp_post, w_in, hgrn_lb_logits, gla_w_gk, gla_b_gk,
           gdn_conv, gdn_a_log, gdn_dt_bias, hgrn_onorm, gla_onorm, gdn_onorm, w_out, w_up, w_down):
    bsz, t_len, _ = x.shape
    n = bsz * t_len
    assert t_len % CHUNK == 0 and n % min(PROJ_TM, n) == 0 and n % min(MLP_TM, n) == 0

    lbp = _lb_params(hgrn_lb_logits)
    w_main, w_forget, w_small = _reorder_w_in(w_in)
    wgk = jnp.pad(gla_w_gk.astype(F32), ((0, 0), (0, LANES - B_RANK), (0, 0)))
    lane_pad = ((0, 0), (SM_A, LANES - SM_A - C_HEADS))
    head_params = jnp.stack([jnp.pad(gdn_a_log.astype(F32), lane_pad),
                             jnp.pad(gdn_dt_bias.astype(F32), lane_pad)], axis=1)
    head_params = jnp.pad(head_params, ((0, 0), (0, SUBLANES - 2), (0, 0)))
    ln_rest = jnp.stack([ln_mix_post, ln_mlp_pre, ln_mlp_post], axis=1).astype(F32)
    ln_rest = jnp.pad(ln_rest, ((0, 0), (0, SUBLANES - 3), (0, 0)))
    wo, wu, wd = w_out.astype(BF16), w_up.astype(BF16), w_down.astype(BF16)

    x2 = x.reshape(n, D_MODEL).astype(F32)
    for layer in range(DEPTH):
        proj, forget, small = _proj(x2, ln_mix_pre[layer][None, :].astype(F32), w_main[layer],
                                    w_forget[layer], w_small[layer])
        proj3 = proj.reshape(bsz, t_len, N_MAIN)
        small3 = small.reshape(bsz, t_len, LANES)
        ya = _mixer_a(proj3, forget.reshape(bsz, t_len, D_MODEL), lbp[layer],
                      hgrn_onorm[layer][None, :].astype(F32))
        yb = _mixer_b(proj3, small3, wgk[layer], gla_b_gk[layer][None, :].astype(F32),
                      gla_onorm[layer][None, :].astype(F32))
        yc = _mixer_c(proj3, small3, gdn_conv[layer].astype(F32), head_params[layer],
                      gdn_onorm[layer][None, :].astype(F32))
        x2 = _out_mlp(x2, ya.reshape(n, D_MODEL), yb.reshape(n, D_MODEL), yc.reshape(n, D_MODEL),
                      wo[layer], wu[layer], wd[layer], ln_rest[layer])
    return x2.reshape(bsz, t_len, D_MODEL).astype(x.dtype)
```

```python
import functools

import jax
import jax.numpy as jnp
from jax import lax
from jax.experimental import pallas as pl
from jax.experimental.pallas import tpu as pltpu

F32 = jnp.float32
BF16 = jnp.bfloat16

D_MODEL = 1024
DEPTH = 4
CHUNK = 64
SUB = 16
N_SUB = CHUNK // SUB
EPS = 1e-6
LOG2_E = 1.4426950408889634
LANES = 128
SUBLANES = 8

A_HEADS, A_DK, A_DV = 8, 128, 128
B_HEADS, B_DK, B_DV = 4, 128, 256
B_KEY = B_HEADS * B_DK
B_RANK = 16
B_GATE_NORM = 16.0
C_HEADS, C_DK, C_DV = 8, 128, 128
C_KEY = C_HEADS * C_DK
C_CONV = 4
C_QKV = 2 * C_KEY + D_MODEL
D_FF = 4 * D_MODEL

BLK_AQ, BLK_AI, BLK_AG = 0, 1, 2
BLK_BQK, BLK_BV, BLK_BG = 3, 4, 5
BLK_CQ, BLK_CK, BLK_CV, BLK_CG = 6, 7, 8, 9
BLK_M0, BLK_M1, BLK_M2 = 10, 11, 12
N_MAIN = 13 * D_MODEL
SM_GK, SM_A, SM_B = 0, B_RANK, B_RANK + C_HEADS

VMEM_LIMIT = 56 * 1024 * 1024


def _dot(a, b):
    return jnp.dot(a, b, preferred_element_type=F32)


def _dot_nt(a, b):
    return lax.dot_general(a, b, (((1,), (1,)), ((), ())), preferred_element_type=F32)


def _dot_tn(a, b):
    return lax.dot_general(a, b, (((0,), (0,)), ((), ())), preferred_element_type=F32)


def _split2(a):
    hi = a.astype(BF16)
    lo = (a - hi.astype(F32)).astype(BF16)
    return hi, lo


def _dot3(a, b):
    ah, al = _split2(a)
    bh, bl = _split2(b)
    return _dot(ah, bh) + _dot(ah, bl) + _dot(al, bh)


def _cumsum_rows(g):
    row = lax.broadcasted_iota(jnp.int32, (CHUNK, CHUNK), 0)
    col = lax.broadcasted_iota(jnp.int32, (CHUNK, CHUNK), 1)
    tril = jnp.where(row >= col, 1.0, 0.0).astype(BF16)
    g1 = g.astype(BF16)
    r1 = g - g1.astype(F32)
    g2 = r1.astype(BF16)
    g3 = (r1 - g2.astype(F32)).astype(BF16)
    return _dot(tril, g1) + _dot(tril, g2) + _dot(tril, g3)


def _sigmoid(x):
    return jax.nn.sigmoid(x)


def _silu(x):
    return x * jax.nn.sigmoid(x)


def _log1p_exp_neg_abs(x):
    return jnp.log(1.0 + jnp.exp(-jnp.abs(x)))


def _log_sigmoid(x):
    return jnp.minimum(x, 0.0) - _log1p_exp_neg_abs(x)


def _rms_rows(x, w):
    return x * lax.rsqrt(jnp.mean(x * x, axis=-1, keepdims=True) + EPS) * w


def _lb_kernel(logits_ref, out_ref):
    lg = logits_ref[...]
    e = jnp.exp(lg - jnp.max(lg, axis=0, keepdims=True))
    p = e / jnp.sum(e, axis=0, keepdims=True)
    cum = p[0:1]
    first = cum
    pad = jnp.zeros((SUBLANES - 3, lg.shape[1]), F32)
    for layer in range(DEPTH):
        if layer > 0:
            cum = cum + p[layer:layer + 1]
        lb = jnp.clip(cum - first, 0.0, 1.0)
        out_ref[layer] = jnp.concatenate([jnp.log(lb), jnp.log1p(-lb), 1.0 - lb, pad], axis=0)


def _lb_params(logits):
    return pl.pallas_call(
        _lb_kernel,
        out_shape=jax.ShapeDtypeStruct((DEPTH, SUBLANES, logits.shape[1]), F32),
        name="hgrn_lower_bounds",
    )(logits.astype(F32))


PROJ_TM = 1024
PROJ_TN = 1024


def _proj_kernel(x_ref, lnw_ref, w_ref, wf_ref, ws_ref, o_ref, of_ref, os_ref, h_ref):
    @pl.when(pl.program_id(1) == 0)
    def _():
        h = _rms_rows(x_ref[...], lnw_ref[...])
        hh, hl = _split2(h)
        h_ref[...] = hh
        wh, wl = _split2(ws_ref[...])
        os_ref[...] = _dot(hh, wh) + _dot(hh, wl) + _dot(hl, wh)
        of_ref[...] = _dot(hh, wf_ref[...])

    o_ref[...] = _dot(h_ref[...], w_ref[...]).astype(BF16)


def _proj(x2, lnw, w_main, w_forget, w_small):
    n = x2.shape[0]
    tm = min(PROJ_TM, n)

    def resident(shape):
        return pl.BlockSpec(shape, lambda i, j: (0, 0), pipeline_mode=pl.Buffered(1))

    return pl.pallas_call(
        _proj_kernel,
        grid=(n // tm, N_MAIN // PROJ_TN),
        in_specs=[
            pl.BlockSpec((tm, D_MODEL), lambda i, j: (i, 0)),
            resident((1, D_MODEL)),
            pl.BlockSpec((D_MODEL, PROJ_TN), lambda i, j: (0, j)),
            resident((D_MODEL, D_MODEL)),
            resident((D_MODEL, LANES)),
        ],
        out_specs=[
            pl.BlockSpec((tm, PROJ_TN), lambda i, j: (i, j)),
            pl.BlockSpec((tm, D_MODEL), lambda i, j: (i, 0)),
            pl.BlockSpec((tm, LANES), lambda i, j: (i, 0)),
        ],
        out_shape=[
            jax.ShapeDtypeStruct((n, N_MAIN), BF16),
            jax.ShapeDtypeStruct((n, D_MODEL), F32),
            jax.ShapeDtypeStruct((n, LANES), F32),
        ],
        scratch_shapes=[pltpu.VMEM((tm, D_MODEL), BF16)],
        compiler_params=pltpu.CompilerParams(
            dimension_semantics=("arbitrary", "arbitrary"), vmem_limit_bytes=VMEM_LIMIT),
        name="rmsnorm_in_proj",
    )(x2, lnw, w_main, w_forget, w_small)


def _chunk_masks():
    row = lax.broadcasted_iota(jnp.int32, (CHUNK, CHUNK), 0)
    col = lax.broadcasted_iota(jnp.int32, (CHUNK, CHUNK), 1)
    sibling = []
    for lvl in range(6):
        bi, bj = row >> lvl, col >> lvl
        sibling.append((bi == bj + 1) & ((bi & 1) == 1))
    return row == col, sibling


def _run_interleaved(gens, group):
    results = [None] * len(gens)
    for lo in range(0, len(gens), group):
        live = list(enumerate(gens))[lo:lo + group]
        while live:
            still = []
            for i, gen in live:
                try:
                    next(gen)
                    still.append((i, gen))
                except StopIteration as stop:
                    results[i] = stop.value
            live = still
    return results


def _gla_head(q, k, v, g, st_ref, kb_ref, h, masks):
    eye, sibling = masks
    width = q.shape[1]
    g = g * LOG2_E
    b = _cumsum_rows(g)
    yield
    b_last = b[CHUNK - 1:CHUNK]
    qd = (q * jnp.exp2(b)).astype(BF16)
    kd = (k * jnp.exp2(b_last - b)).astype(BF16)
    s_decay = jnp.exp2(b_last)
    vb = v.astype(BF16)
    qb = q.astype(BF16)
    kb = k.astype(BF16)

    kb_ref[0] = b - g
    kb_ref[1] = b

    subl = lax.broadcasted_iota(jnp.int32, (SUBLANES, width), 0)

    def block_rows(idx, s, off):
        cache, pieces = {}, []

        def bcast(r):
            if r not in cache:
                cache[r] = jnp.broadcast_to(kb_ref[idx, pl.ds(r, 1), :], (SUBLANES, width))
            return cache[r]

        for base in range(0, CHUNK, SUBLANES):
            if s >= SUBLANES:
                pieces.append(bcast(s * (base // s) + off))
            else:
                n_blk = SUBLANES // s
                piece = bcast(base + (n_blk - 1) * s + off)
                for t in reversed(range(n_blk - 1)):
                    piece = jnp.where(subl < (t + 1) * s, bcast(base + t * s + off), piece)
                pieces.append(piece)
        return jnp.concatenate(pieces, axis=0)

    q_lv = [(q * jnp.exp2(g)).astype(BF16)]
    k_lv = [kb]
    odd = (subl & 1) == 1
    g_prev, g_next = [], []
    for base in range(0, CHUNK, SUBLANES):
        piece = g[base:base + SUBLANES]
        g_prev.append(jnp.where(odd, pltpu.roll(piece, 1, 0), 0.0))
        g_next.append(jnp.where(odd, 0.0, pltpu.roll(piece, SUBLANES - 1, 0)))
    q_lv.append((q * jnp.exp2(g + jnp.concatenate(g_prev, axis=0))).astype(BF16))
    k_lv.append((k * jnp.exp2(jnp.concatenate(g_next, axis=0))).astype(BF16))
    for lvl in range(2, 6):
        s = 1 << lvl
        q_lv.append((q * jnp.exp2(b - block_rows(0, s, 0))).astype(BF16))
        k_lv.append((k * jnp.exp2(block_rows(1, s, s - 1) - b)).astype(BF16))

    yield
    first = _dot_nt(jnp.concatenate([q_lv[0], qb], axis=0), kb)
    prods = [_dot_nt(q_lv[lvl], k_lv[lvl]) for lvl in range(1, 6)]
    st = st_ref[h]
    o_inter = _dot_nt(qd, st.astype(BF16))
    st_ref[h] = st * s_decay + _dot_tn(vb, kd)
    yield
    a = jnp.where(sibling[0], first[:CHUNK], jnp.where(eye, first[CHUNK:], 0.0))
    for lvl in range(1, 6):
        a = jnp.where(sibling[lvl], prods[lvl - 1], a)
    ab = a.astype(BF16)
    yield
    return o_inter + _dot(ab, vb)


def _head_out(outs, onorm, gate, merge):
    y = jnp.concatenate([_rms_rows(o, onorm) for o in outs], axis=-1)
    return y * _silu(gate) * _sigmoid(merge)


def _f32(ref, rows=slice(None), lanes=slice(None)):
    return ref[rows, lanes].astype(F32)


def _mixer_a_kernel(aq_ref, ai_ref, ag_ref, m_ref, af_ref, lbp_ref, onorm_ref, y_ref, st_ref, kb_ref):
    @pl.when(pl.program_id(1) == 0)
    def _():
        st_ref[...] = jnp.zeros_like(st_ref)

    masks = _chunk_masks()

    def head(c, h):
        rows = slice(c * CHUNK, (c + 1) * CHUNK)
        hs = slice(h * A_DK, (h + 1) * A_DK)
        z = af_ref[rows, hs]
        log_lb, log_1m_lb, one_m_lb = lbp_ref[0:1, hs], lbp_ref[1:2, hs], lbp_ref[2:3, hs]
        t = log_1m_lb + _log_sigmoid(z)
        g = jnp.maximum(log_lb, t) + _log1p_exp_neg_abs(log_lb - t)
        k = one_m_lb * _sigmoid(-z)
        q = _silu(_f32(aq_ref, rows, hs))
        o = yield from _gla_head(q, k, _f32(ai_ref, rows, hs), g, st_ref, kb_ref.at[c * A_HEADS + h], h,
                                 masks)
        yield
        y = _head_out([o], onorm_ref[...], _f32(ag_ref, rows, hs), _f32(m_ref, rows, hs))
        y_ref[rows, hs] = y.astype(BF16)

    _run_interleaved([head(c, h) for c in range(aq_ref.shape[0] // CHUNK) for h in range(A_HEADS)],
                     GLA_GROUP)


def _mixer_b_kernel(qk_ref, v_ref, bg_ref, m_ref, sm_ref, wgk_ref, bgk_ref, onorm_ref, y_ref, st_ref, kb_ref):
    @pl.when(pl.program_id(1) == 0)
    def _():
        st_ref[...] = jnp.zeros_like(st_ref)

    masks = _chunk_masks()

    def head(c, h):
        rows = slice(c * CHUNK, (c + 1) * CHUNK)
        hs = slice(h * B_DK, (h + 1) * B_DK)
        vs = slice(h * B_DV, (h + 1) * B_DV)
        q = _f32(qk_ref, rows, hs) * (B_DK ** -0.5)
        k = _f32(qk_ref, rows, slice(B_KEY + h * B_DK, B_KEY + (h + 1) * B_DK))
        smh, sml = _split2(sm_ref[rows, :])
        wh, wl = _split2(wgk_ref[:, hs])
        gk = _dot(smh, wh) + _dot(smh, wl) + _dot(sml, wh) + bgk_ref[:, hs]
        yield
        g = _log_sigmoid(gk) / B_GATE_NORM
        o = yield from _gla_head(q, k, _f32(v_ref, rows, vs), g, st_ref, kb_ref.at[c * B_HEADS + h], h,
                                 masks)
        yield
        y = _head_out([o], onorm_ref[...], _f32(bg_ref, rows, vs), _f32(m_ref, rows, vs))
        y_ref[rows, vs] = y.astype(BF16)

    _run_interleaved([head(c, h) for c in range(qk_ref.shape[0] // CHUNK) for h in range(B_HEADS)],
                     GLA_GROUP)


def _tok_index(b, t, blk):
    return (b, t, blk)


def _full_spec(shape):
    return pl.BlockSpec(shape, lambda b, t: (0,) * len(shape))


def _mixer_call(kernel, name, proj3, tok_blocks, extra_inputs, extra_specs, scratch, tb=CHUNK):
    bsz, t_len, _ = proj3.shape
    assert t_len % tb == 0 and tb % CHUNK == 0
    return pl.pallas_call(
        kernel,
        grid=(bsz, t_len // tb),
        in_specs=[pl.BlockSpec((None, tb, D_MODEL), functools.partial(_tok_index, blk=blk))
                  for blk in tok_blocks] + extra_specs,
        out_specs=pl.BlockSpec((None, tb, D_MODEL), lambda b, t: (b, t, 0)),
        out_shape=jax.ShapeDtypeStruct((bsz, t_len, D_MODEL), BF16),
        scratch_shapes=scratch,
        compiler_params=pltpu.CompilerParams(
            dimension_semantics=("arbitrary", "arbitrary"), vmem_limit_bytes=VMEM_LIMIT),
        name=name,
    )(*([proj3] * len(tok_blocks)), *extra_inputs)


GLA_TB = 128
GLA_GROUP = 16


def _mixer_a(proj3, forget3, lbp, onorm):
    tb = min(GLA_TB, proj3.shape[1])
    return _mixer_call(
        _mixer_a_kernel, "hgrn2_mixer", proj3,
        [BLK_AQ, BLK_AI, BLK_AG, BLK_M0],
        [forget3, lbp, onorm],
        [pl.BlockSpec((None, tb, D_MODEL), lambda b, t: (b, t, 0)),
         _full_spec((SUBLANES, D_MODEL)), _full_spec((1, A_DV))],
        [pltpu.VMEM((A_HEADS, A_DV, A_DK), F32),
         pltpu.VMEM((tb // CHUNK * A_HEADS, 2, CHUNK, A_DK), F32)],
        tb=tb)


def _mixer_b(proj3, small3, wgk, bgk, onorm):
    tb = min(GLA_TB, proj3.shape[1])
    return _mixer_call(
        _mixer_b_kernel, "gla_mixer", proj3,
        [BLK_BQK, BLK_BV, BLK_BG, BLK_M1],
        [small3, wgk, bgk, onorm],
        [pl.BlockSpec((None, tb, LANES), lambda b, t: (b, t, 0)),
         _full_spec((LANES, B_KEY)), _full_spec((1, B_KEY)), _full_spec((1, B_DV))],
        [pltpu.VMEM((B_HEADS, B_DV, B_DK), F32),
         pltpu.VMEM((tb // CHUNK * B_HEADS, 2, CHUNK, B_DK), F32)],
        tb=tb)


def _l2norm_rows(x):
    return x * lax.rsqrt(jnp.sum(x * x, axis=-1, keepdims=True) + EPS)


def _mixer_c_kernel(cq_ref, ck_ref, cv_ref, cg_ref, m_ref, sm_ref, conv_ref, hp_ref, onorm_ref,
                    y_ref, st_ref, xb_ref):
    hist = SUBLANES

    @pl.when(pl.program_id(1) == 0)
    def _():
        st_ref[...] = jnp.zeros_like(st_ref)
        xb_ref[0:hist, :] = jnp.zeros((hist, C_QKV), F32)

    tb = cq_ref.shape[0]
    xb_ref[hist:hist + tb, 0:C_KEY] = _f32(cq_ref)
    xb_ref[hist:hist + tb, C_KEY:2 * C_KEY] = _f32(ck_ref)
    xb_ref[hist:hist + tb, 2 * C_KEY:C_QKV] = _f32(cv_ref)

    row = lax.broadcasted_iota(jnp.int32, (CHUNK, CHUNK), 0)
    col = lax.broadcasted_iota(jnp.int32, (CHUNK, CHUNK), 1)
    eye = jnp.where(row == col, 1.0, 0.0).astype(F32)
    causal = row >= col
    merge_mask = []
    for lvl in range(6):
        bi, bj = row >> lvl, col >> lvl
        merge_mask.append((bi == bj + 1) & ((bi & 1) == 1))

    chains = []
    shared = []
    for c in range(tb // CHUNK):
        r0 = hist + c * CHUNK
        acc = xb_ref[r0:r0 + CHUNK, :] * conv_ref[C_CONV - 1:C_CONV, :]
        for j in range(C_CONV - 1):
            shift = C_CONV - 1 - j
            acc = acc + xb_ref[r0 - shift:r0 - shift + CHUNK, :] * conv_ref[j:j + 1, :]
        qkv = _silu(acc)
        sm = sm_ref[c * CHUNK:(c + 1) * CHUNK, :]
        log_a = -jnp.exp(hp_ref[0:1, :]) * jax.nn.softplus(sm + hp_ref[1:2, :])
        beta_all = _sigmoid(sm)
        b_all = _cumsum_rows(log_a)
        b_all_t = b_all.T
        e_all = jnp.exp(b_all)
        d_all = jnp.exp(b_all[CHUNK - 1:CHUNK] - b_all)
        shared.append(e_all[CHUNK - 1:CHUNK])
        for h in range(C_HEADS):
            chains.append(dict(
                q=_l2norm_rows(qkv[:, h * C_DK:(h + 1) * C_DK]) * (C_DK ** -0.5),
                k=_l2norm_rows(qkv[:, C_KEY + h * C_DK:C_KEY + (h + 1) * C_DK]),
                v=qkv[:, 2 * C_KEY + h * C_DV:2 * C_KEY + (h + 1) * C_DV],
                b_col=b_all[:, SM_A + h:SM_A + h + 1],
                b_row=b_all_t[SM_A + h:SM_A + h + 1, :],
                beta=beta_all[:, SM_B + h:SM_B + h + 1],
                e_b=e_all[:, SM_A + h:SM_A + h + 1],
                d_b=d_all[:, SM_A + h:SM_A + h + 1]))
    xb_ref[0:hist, :] = xb_ref[tb:tb + hist, :]

    for ch in chains:
        ch["decay"] = jnp.where(causal, jnp.exp(jnp.minimum(ch["b_col"] - ch["b_row"], 0.0)), 0.0)
        ch["kb"] = ch["k"].astype(BF16)
        ch["qb"] = ch["q"].astype(BF16)
    for ch in chains:
        ch["kk"] = _dot_nt(ch["kb"], ch["kb"])
        ch["qk"] = _dot_nt(ch["qb"], ch["kb"])
    for ch in chains:
        ch["lower"] = ch["beta"] * ch["kk"] * ch["decay"]
        ch["qkd"] = (ch["qk"] * ch["decay"]).astype(BF16)
        ch["x"] = eye - jnp.where(merge_mask[0], ch["lower"], 0.0)
    for lvl in range(1, 6):
        for ch in chains:
            cpart = jnp.where(merge_mask[lvl], ch["lower"], 0.0).astype(BF16)
            ch["xb"] = ch["x"].astype(BF16)
            ch["y"] = _dot(cpart, ch["xb"]).astype(BF16)
        for ch in chains:
            ch["x"] = ch["x"] - _dot(ch["xb"], ch["y"])
    for ch in chains:
        rhs = jnp.concatenate([ch["v"] * ch["beta"], ch["k"] * (ch["beta"] * ch["e_b"])], axis=-1)
        ch["sol"] = _dot(ch["x"].astype(BF16), rhs.astype(BF16))
        ch["qe"] = (ch["q"] * ch["e_b"]).astype(BF16)
        ch["kdec"] = (ch["k"] * ch["d_b"]).astype(BF16)

    for c in range(tb // CHUNK):
        mine = chains[c * C_HEADS:(c + 1) * C_HEADS]
        sts = [st_ref[h] for h in range(C_HEADS)]
        stb = [s.astype(BF16) for s in sts]
        v_new = [ch["sol"][:, :C_DV] - _dot_nt(ch["sol"][:, C_DV:].astype(BF16), sb)
                 for ch, sb in zip(mine, stb)]
        vnb = [vn.astype(BF16) for vn in v_new]
        outs = [_dot_nt(ch["qe"], sb) + _dot(ch["qkd"], vn) for ch, sb, vn in zip(mine, stb, vnb)]
        for h in range(C_HEADS):
            a_last = shared[c][:, SM_A + h:SM_A + h + 1]
            st_ref[h] = a_last * sts[h] + _dot_tn(vnb[h], mine[h]["kdec"])
        rs = slice(c * CHUNK, (c + 1) * CHUNK)
        y_ref[rs, :] = _head_out(outs, onorm_ref[...], _f32(cg_ref, rs), _f32(m_ref, rs)).astype(BF16)


GDN_TB = 128


def _mixer_c(proj3, small3, conv, head_params, onorm):
    tb = min(GDN_TB, proj3.shape[1])
    return _mixer_call(
        _mixer_c_kernel, "gated_deltanet_mixer", proj3,
        [BLK_CQ, BLK_CK, BLK_CV, BLK_CG, BLK_M2],
        [small3, conv, head_params, onorm],
        [pl.BlockSpec((None, tb, LANES), lambda b, t: (b, t, 0)),
         _full_spec((C_CONV, C_QKV)), _full_spec((SUBLANES, LANES)), _full_spec((1, C_DV))],
        [pltpu.VMEM((C_HEADS, C_DV, C_DK), F32), pltpu.VMEM((SUBLANES + tb, C_QKV), F32)],
        tb=tb)


MLP_TM = 256


def _out_mlp_kernel(x_ref, ya_ref, yb_ref, yc_ref, wo_ref, wu_ref, wd_ref, ln_ref, o_ref):
    y = (_f32(ya_ref) + _f32(yb_ref) + _f32(yc_ref)).astype(BF16)
    mix = _dot(y, wo_ref[...])
    x1 = x_ref[...] + _rms_rows(mix, ln_ref[0:1, :])
    h = _rms_rows(x1, ln_ref[1:2, :]).astype(BF16)
    up = _dot(h, wu_ref[...])
    act = jnp.square(jnp.maximum(up, 0.0)).astype(BF16)
    down = _dot(act, wd_ref[...])
    o_ref[...] = x1 + _rms_rows(down, ln_ref[2:3, :])


def _out_mlp(x2, ya, yb, yc, wo, wu, wd, ln):
    n = x2.shape[0]
    tm = min(MLP_TM, n)
    tok = pl.BlockSpec((tm, D_MODEL), lambda i: (i, 0))

    def resident(shape):
        return pl.BlockSpec(shape, lambda i: (0, 0), pipeline_mode=pl.Buffered(1))

    return pl.pallas_call(
        _out_mlp_kernel,
        grid=(n // tm,),
        in_specs=[tok, tok, tok, tok,
                  resident((D_MODEL, D_MODEL)), resident((D_MODEL, D_FF)), resident((D_FF, D_MODEL)),
                  resident((SUBLANES, D_MODEL))],
        out_specs=tok,
        out_shape=jax.ShapeDtypeStruct((n, D_MODEL), F32),
        compiler_params=pltpu.CompilerParams(
            dimension_semantics=("arbitrary",), vmem_limit_bytes=VMEM_LIMIT),
        name="merge_out_proj_mlp",
    )(x2, ya, yb, yc, wo, wu, wd, ln)


def _reorder_w_in(w_in):
    o_bgk = 4 * D_MODEL + 2 * B_KEY + D_MODEL
    o_bg = o_bgk + B_RANK
    o_cqkv = o_bg + D_MODEL
    o_ca = o_cqkv + C_QKV
    o_cg = o_ca + 2 * C_HEADS
    main = jnp.concatenate([w_in[..., :D_MODEL], w_in[..., 2 * D_MODEL:o_bgk], w_in[..., o_bg:o_ca],
                            w_in[..., o_cg:]], axis=-1)
    forget = w_in[..., D_MODEL:2 * D_MODEL]
    pad = jnp.zeros(w_in.shape[:-1] + (LANES - B_RANK - 2 * C_HEADS,), w_in.dtype)
    small = jnp.concatenate([w_in[..., o_bgk:o_bg], w_in[..., o_ca:o_cg], pad], axis=-1)
    return main.astype(BF16), forget.astype(BF16), small.astype(F32)


def kernel(x, ln_mix_pre, ln_mix_post, ln_mlp_pre, ln_mlp_post, w_in, hgrn_lb_logits, gla_w_gk, gla_b_gk,
           gdn_conv, gdn_a_log, gdn_dt_bias, hgrn_onorm, gla_onorm, gdn_onorm, w_out, w_up, w_down):
    bsz, t_len, _ = x.shape
    n = bsz * t_len
    assert t_len % CHUNK == 0 and n % min(PROJ_TM, n) == 0 and n % min(MLP_TM, n) == 0

    lbp = _lb_params(hgrn_lb_logits)
    w_main, w_forget, w_small = _reorder_w_in(w_in)
    wgk = jnp.pad(gla_w_gk.astype(F32), ((0, 0), (0, LANES - B_RANK), (0, 0)))
    lane_pad = ((0, 0), (SM_A, LANES - SM_A - C_HEADS))
    head_params = jnp.stack([jnp.pad(gdn_a_log.astype(F32), lane_pad),
                             jnp.pad(gdn_dt_bias.astype(F32), lane_pad)], axis=1)
    head_params = jnp.pad(head_params, ((0, 0), (0, SUBLANES - 2), (0, 0)))
    ln_rest = jnp.stack([ln_mix_post, ln_mlp_pre, ln_mlp_post], axis=1).astype(F32)
    ln_rest = jnp.pad(ln_rest, ((0, 0), (0, SUBLANES - 3), (0, 0)))
    wo, wu, wd = w_out.astype(BF16), w_up.astype(BF16), w_down.astype(BF16)

    x2 = x.reshape(n, D_MODEL).astype(F32)
    for layer in range(DEPTH):
        proj, forget, small = _proj(x2, ln_mix_pre[layer][None, :].astype(F32), w_main[layer],
                                    w_forget[layer], w_small[layer])
        proj3 = proj.reshape(bsz, t_len, N_MAIN)
        small3 = small.reshape(bsz, t_len, LANES)
        ya = _mixer_a(proj3, forget.reshape(bsz, t_len, D_MODEL), lbp[layer],
                      hgrn_onorm[layer][None, :].astype(F32))
        yb = _mixer_b(proj3, small3, wgk[layer], gla_b_gk[layer][None, :].astype(F32),
                      gla_onorm[layer][None, :].astype(F32))
        yc = _mixer_c(proj3, small3, gdn_conv[layer].astype(F32), head_params[layer],
                      gdn_onorm[layer][None, :].astype(F32))
        x2 = _out_mlp(x2, ya.reshape(n, D_MODEL), yb.reshape(n, D_MODEL), yc.reshape(n, D_MODEL),
                      wo[layer], wu[layer], wd[layer], ln_rest[layer])
    return x2.reshape(bsz, t_len, D_MODEL).astype(x.dtype)
```

```python
import functools

import jax
import jax.numpy as jnp
from jax import lax
from jax.experimental import pallas as pl
from jax.experimental.pallas import tpu as pltpu

F32 = jnp.float32
BF16 = jnp.bfloat16

D_MODEL = 1024
DEPTH = 4
CHUNK = 64
SUB = 16
N_SUB = CHUNK // SUB
EPS = 1e-6
LOG2_E = 1.4426950408889634
LANES = 128
SUBLANES = 8

A_HEADS, A_DK, A_DV = 8, 128, 128
B_HEADS, B_DK, B_DV = 4, 128, 256
B_KEY = B_HEADS * B_DK
B_RANK = 16
B_GATE_NORM = 16.0
C_HEADS, C_DK, C_DV = 8, 128, 128
C_KEY = C_HEADS * C_DK
C_CONV = 4
C_QKV = 2 * C_KEY + D_MODEL
D_FF = 4 * D_MODEL

BLK_AQ, BLK_AI, BLK_AG = 0, 1, 2
BLK_BQK, BLK_BV, BLK_BG = 3, 4, 5
BLK_CQ, BLK_CK, BLK_CV, BLK_CG = 6, 7, 8, 9
BLK_M0, BLK_M1, BLK_M2 = 10, 11, 12
N_MAIN = 13 * D_MODEL
SM_GK, SM_A, SM_B = 0, B_RANK, B_RANK + C_HEADS

VMEM_LIMIT = 56 * 1024 * 1024


def _dot(a, b):
    return jnp.dot(a, b, preferred_element_type=F32)


def _dot_nt(a, b):
    return lax.dot_general(a, b, (((1,), (1,)), ((), ())), preferred_element_type=F32)


def _dot_tn(a, b):
    return lax.dot_general(a, b, (((0,), (0,)), ((), ())), preferred_element_type=F32)


def _split2(a):
    hi = a.astype(BF16)
    lo = (a - hi.astype(F32)).astype(BF16)
    return hi, lo


def _dot3(a, b):
    ah, al = _split2(a)
    bh, bl = _split2(b)
    return _dot(ah, bh) + _dot(ah, bl) + _dot(al, bh)


def _cumsum_rows(g):
    row = lax.broadcasted_iota(jnp.int32, (CHUNK, CHUNK), 0)
    col = lax.broadcasted_iota(jnp.int32, (CHUNK, CHUNK), 1)
    tril = jnp.where(row >= col, 1.0, 0.0).astype(BF16)
    g1 = g.astype(BF16)
    r1 = g - g1.astype(F32)
    g2 = r1.astype(BF16)
    g3 = (r1 - g2.astype(F32)).astype(BF16)
    return _dot(tril, g1) + _dot(tril, g2) + _dot(tril, g3)


def _sigmoid(x):
    return jax.nn.sigmoid(x)


def _silu(x):
    return x * jax.nn.sigmoid(x)


def _log1p_exp_neg_abs(x):
    return jnp.log(1.0 + jnp.exp(-jnp.abs(x)))


def _log_sigmoid(x):
    return jnp.minimum(x, 0.0) - _log1p_exp_neg_abs(x)


def _rms_rows(x, w):
    return x * lax.rsqrt(jnp.mean(x * x, axis=-1, keepdims=True) + EPS) * w


def _lb_kernel(logits_ref, out_ref):
    lg = logits_ref[...]
    e = jnp.exp(lg - jnp.max(lg, axis=0, keepdims=True))
    p = e / jnp.sum(e, axis=0, keepdims=True)
    cum = p[0:1]
    first = cum
    pad = jnp.zeros((SUBLANES - 3, lg.shape[1]), F32)
    for layer in range(DEPTH):
        if layer > 0:
            cum = cum + p[layer:layer + 1]
        lb = jnp.clip(cum - first, 0.0, 1.0)
        out_ref[layer] = jnp.concatenate([jnp.log(lb), jnp.log1p(-lb), 1.0 - lb, pad], axis=0)


def _lb_params(logits):
    return pl.pallas_call(
        _lb_kernel,
        out_shape=jax.ShapeDtypeStruct((DEPTH, SUBLANES, logits.shape[1]), F32),
        name="hgrn_lower_bounds",
    )(logits.astype(F32))


PROJ_TM = 1024
PROJ_TN = 1024


def _proj_kernel(x_ref, lnw_ref, w_ref, wf_ref, ws_ref, o_ref, of_ref, os_ref, h_ref):
    @pl.when(pl.program_id(1) == 0)
    def _():
        h = _rms_rows(x_ref[...], lnw_ref[...])
        hh, hl = _split2(h)
        h_ref[...] = hh
        wh, wl = _split2(ws_ref[...])
        os_ref[...] = _dot(hh, wh) + _dot(hh, wl) + _dot(hl, wh)
        of_ref[...] = _dot(hh, wf_ref[...])

    o_ref[...] = _dot(h_ref[...], w_ref[...]).astype(BF16)


def _proj(x2, lnw, w_main, w_forget, w_small):
    n = x2.shape[0]
    tm = min(PROJ_TM, n)

    def resident(shape):
        return pl.BlockSpec(shape, lambda i, j: (0, 0), pipeline_mode=pl.Buffered(1))

    return pl.pallas_call(
        _proj_kernel,
        grid=(n // tm, N_MAIN // PROJ_TN),
        in_specs=[
            pl.BlockSpec((tm, D_MODEL), lambda i, j: (i, 0)),
            resident((1, D_MODEL)),
            pl.BlockSpec((D_MODEL, PROJ_TN), lambda i, j: (0, j)),
            resident((D_MODEL, D_MODEL)),
            resident((D_MODEL, LANES)),
        ],
        out_specs=[
            pl.BlockSpec((tm, PROJ_TN), lambda i, j: (i, j)),
            pl.BlockSpec((tm, D_MODEL), lambda i, j: (i, 0)),
            pl.BlockSpec((tm, LANES), lambda i, j: (i, 0)),
        ],
        out_shape=[
            jax.ShapeDtypeStruct((n, N_MAIN), BF16),
            jax.ShapeDtypeStruct((n, D_MODEL), F32),
            jax.ShapeDtypeStruct((n, LANES), F32),
        ],
        scratch_shapes=[pltpu.VMEM((tm, D_MODEL), BF16)],
        compiler_params=pltpu.CompilerParams(
            dimension_semantics=("arbitrary", "arbitrary"), vmem_limit_bytes=VMEM_LIMIT),
        name="rmsnorm_in_proj",
    )(x2, lnw, w_main, w_forget, w_small)


def _chunk_masks():
    row = lax.broadcasted_iota(jnp.int32, (CHUNK, CHUNK), 0)
    col = lax.broadcasted_iota(jnp.int32, (CHUNK, CHUNK), 1)
    sibling = []
    for lvl in range(6):
        bi, bj = row >> lvl, col >> lvl
        sibling.append((bi == bj + 1) & ((bi & 1) == 1))
    return row == col, sibling


def _run_interleaved(gens, group, delay=None):
    results = [None] * len(gens)
    for lo in range(0, len(gens), group):
        live = list(enumerate(gens))[lo:lo + group]
        rnd = 0
        while live:
            still = []
            for i, gen in live:
                if delay is not None and rnd < delay[i]:
                    still.append((i, gen))
                    continue
                try:
                    next(gen)
                    still.append((i, gen))
                except StopIteration as stop:
                    results[i] = stop.value
            live = still
            rnd += 1
    return results


def _gla_head(q, k, v, g, st_ref, kb_ref, h, masks):
    eye, sibling = masks
    width = q.shape[1]
    g = g * LOG2_E
    b = _cumsum_rows(g)
    yield
    b_last = b[CHUNK - 1:CHUNK]
    qd = (q * jnp.exp2(b)).astype(BF16)
    kd = (k * jnp.exp2(b_last - b)).astype(BF16)
    s_decay = jnp.exp2(b_last)
    vb = v.astype(BF16)
    qb = q.astype(BF16)
    kb = k.astype(BF16)

    kb_ref[0] = b - g
    kb_ref[1] = b

    subl = lax.broadcasted_iota(jnp.int32, (SUBLANES, width), 0)

    def block_rows(idx, s, off):
        cache, pieces = {}, []

        def bcast(r):
            if r not in cache:
                cache[r] = jnp.broadcast_to(kb_ref[idx, pl.ds(r, 1), :], (SUBLANES, width))
            return cache[r]

        for base in range(0, CHUNK, SUBLANES):
            if s >= SUBLANES:
                pieces.append(bcast(s * (base // s) + off))
            else:
                n_blk = SUBLANES // s
                piece = bcast(base + (n_blk - 1) * s + off)
                for t in reversed(range(n_blk - 1)):
                    piece = jnp.where(subl < (t + 1) * s, bcast(base + t * s + off), piece)
                pieces.append(piece)
        return jnp.concatenate(pieces, axis=0)

    q_lv = [(q * jnp.exp2(g)).astype(BF16)]
    k_lv = [kb]
    odd = (subl & 1) == 1
    g_prev, g_next = [], []
    for base in range(0, CHUNK, SUBLANES):
        piece = g[base:base + SUBLANES]
        g_prev.append(jnp.where(odd, pltpu.roll(piece, 1, 0), 0.0))
        g_next.append(jnp.where(odd, 0.0, pltpu.roll(piece, SUBLANES - 1, 0)))
    q_lv.append((q * jnp.exp2(g + jnp.concatenate(g_prev, axis=0))).astype(BF16))
    k_lv.append((k * jnp.exp2(jnp.concatenate(g_next, axis=0))).astype(BF16))
    for lvl in range(2, 6):
        s = 1 << lvl
        q_lv.append((q * jnp.exp2(b - block_rows(0, s, 0))).astype(BF16))
        k_lv.append((k * jnp.exp2(block_rows(1, s, s - 1) - b)).astype(BF16))

    yield
    first = _dot_nt(jnp.concatenate([q_lv[0], qb], axis=0), kb)
    prods = [_dot_nt(q_lv[lvl], k_lv[lvl]) for lvl in range(1, 6)]
    st = st_ref[h]
    o_inter = _dot_nt(qd, st.astype(BF16))
    st_ref[h] = st * s_decay + _dot_tn(vb, kd)
    yield
    a = jnp.where(sibling[0], first[:CHUNK], jnp.where(eye, first[CHUNK:], 0.0))
    for lvl in range(1, 6):
        a = jnp.where(sibling[lvl], prods[lvl - 1], a)
    ab = a.astype(BF16)
    yield
    return o_inter + _dot(ab, vb)


def _head_out(outs, onorm, gate, merge):
    y = jnp.concatenate([_rms_rows(o, onorm) for o in outs], axis=-1)
    return y * _silu(gate) * _sigmoid(merge)


def _f32(ref, rows=slice(None), lanes=slice(None)):
    return ref[rows, lanes].astype(F32)


def _mixer_a_kernel(aq_ref, ai_ref, ag_ref, m_ref, af_ref, lbp_ref, onorm_ref, y_ref, st_ref, kb_ref):
    @pl.when(pl.program_id(1) == 0)
    def _():
        st_ref[...] = jnp.zeros_like(st_ref)

    masks = _chunk_masks()

    def head(c, h):
        rows = slice(c * CHUNK, (c + 1) * CHUNK)
        hs = slice(h * A_DK, (h + 1) * A_DK)
        z = af_ref[rows, hs]
        log_lb, log_1m_lb, one_m_lb = lbp_ref[0:1, hs], lbp_ref[1:2, hs], lbp_ref[2:3, hs]
        t = log_1m_lb + _log_sigmoid(z)
        g = jnp.maximum(log_lb, t) + _log1p_exp_neg_abs(log_lb - t)
        k = one_m_lb * _sigmoid(-z)
        q = _silu(_f32(aq_ref, rows, hs))
        o = yield from _gla_head(q, k, _f32(ai_ref, rows, hs), g, st_ref, kb_ref.at[c * A_HEADS + h], h,
                                 masks)
        yield
        y = _head_out([o], onorm_ref[...], _f32(ag_ref, rows, hs), _f32(m_ref, rows, hs))
        y_ref[rows, hs] = y.astype(BF16)

    _run_interleaved([head(c, h) for c in range(aq_ref.shape[0] // CHUNK) for h in range(A_HEADS)],
                     GLA_GROUP)


def _mixer_b_kernel(qk_ref, v_ref, bg_ref, m_ref, sm_ref, wgk_ref, bgk_ref, onorm_ref, y_ref, st_ref, kb_ref):
    @pl.when(pl.program_id(1) == 0)
    def _():
        st_ref[...] = jnp.zeros_like(st_ref)

    masks = _chunk_masks()

    def head(c, h):
        rows = slice(c * CHUNK, (c + 1) * CHUNK)
        hs = slice(h * B_DK, (h + 1) * B_DK)
        vs = slice(h * B_DV, (h + 1) * B_DV)
        q = _f32(qk_ref, rows, hs) * (B_DK ** -0.5)
        k = _f32(qk_ref, rows, slice(B_KEY + h * B_DK, B_KEY + (h + 1) * B_DK))
        smh, sml = _split2(sm_ref[rows, :])
        wh, wl = _split2(wgk_ref[:, hs])
        gk = _dot(smh, wh) + _dot(smh, wl) + _dot(sml, wh) + bgk_ref[:, hs]
        yield
        g = _log_sigmoid(gk) / B_GATE_NORM
        o = yield from _gla_head(q, k, _f32(v_ref, rows, vs), g, st_ref, kb_ref.at[c * B_HEADS + h], h,
                                 masks)
        yield
        y = _head_out([o], onorm_ref[...], _f32(bg_ref, rows, vs), _f32(m_ref, rows, vs))
        y_ref[rows, vs] = y.astype(BF16)

    _run_interleaved([head(c, h) for c in range(qk_ref.shape[0] // CHUNK) for h in range(B_HEADS)],
                     GLA_GROUP)


def _tok_index(b, t, blk):
    return (b, t, blk)


def _full_spec(shape):
    return pl.BlockSpec(shape, lambda b, t: (0,) * len(shape))


def _mixer_call(kernel, name, proj3, tok_blocks, extra_inputs, extra_specs, scratch, tb=CHUNK):
    bsz, t_len, _ = proj3.shape
    assert t_len % tb == 0 and tb % CHUNK == 0
    return pl.pallas_call(
        kernel,
        grid=(bsz, t_len // tb),
        in_specs=[pl.BlockSpec((None, tb, D_MODEL), functools.partial(_tok_index, blk=blk))
                  for blk in tok_blocks] + extra_specs,
        out_specs=pl.BlockSpec((None, tb, D_MODEL), lambda b, t: (b, t, 0)),
        out_shape=jax.ShapeDtypeStruct((bsz, t_len, D_MODEL), BF16),
        scratch_shapes=scratch,
        compiler_params=pltpu.CompilerParams(
            dimension_semantics=("arbitrary", "arbitrary"), vmem_limit_bytes=VMEM_LIMIT),
        name=name,
    )(*([proj3] * len(tok_blocks)), *extra_inputs)


GLA_TB = 256
GLA_GROUP = 16


def _mixer_a(proj3, forget3, lbp, onorm):
    tb = min(GLA_TB, proj3.shape[1])
    return _mixer_call(
        _mixer_a_kernel, "hgrn2_mixer", proj3,
        [BLK_AQ, BLK_AI, BLK_AG, BLK_M0],
        [forget3, lbp, onorm],
        [pl.BlockSpec((None, tb, D_MODEL), lambda b, t: (b, t, 0)),
         _full_spec((SUBLANES, D_MODEL)), _full_spec((1, A_DV))],
        [pltpu.VMEM((A_HEADS, A_DV, A_DK), F32),
         pltpu.VMEM((tb // CHUNK * A_HEADS, 2, CHUNK, A_DK), F32)],
        tb=tb)


def _mixer_b(proj3, small3, wgk, bgk, onorm):
    tb = min(GLA_TB, proj3.shape[1])
    return _mixer_call(
        _mixer_b_kernel, "gla_mixer", proj3,
        [BLK_BQK, BLK_BV, BLK_BG, BLK_M1],
        [small3, wgk, bgk, onorm],
        [pl.BlockSpec((None, tb, LANES), lambda b, t: (b, t, 0)),
         _full_spec((LANES, B_KEY)), _full_spec((1, B_KEY)), _full_spec((1, B_DV))],
        [pltpu.VMEM((B_HEADS, B_DV, B_DK), F32),
         pltpu.VMEM((tb // CHUNK * B_HEADS, 2, CHUNK, B_DK), F32)],
        tb=tb)


def _l2norm_rows(x):
    return x * lax.rsqrt(jnp.sum(x * x, axis=-1, keepdims=True) + EPS)


def _mixer_c_kernel(cq_ref, ck_ref, cv_ref, cg_ref, m_ref, sm_ref, conv_ref, hp_ref, onorm_ref,
                    y_ref, st_ref, xb_ref):
    hist = SUBLANES

    @pl.when(pl.program_id(1) == 0)
    def _():
        st_ref[...] = jnp.zeros_like(st_ref)
        xb_ref[0:hist, :] = jnp.zeros((hist, C_QKV), F32)

    tb = cq_ref.shape[0]
    xb_ref[hist:hist + tb, 0:C_KEY] = _f32(cq_ref)
    xb_ref[hist:hist + tb, C_KEY:2 * C_KEY] = _f32(ck_ref)
    xb_ref[hist:hist + tb, 2 * C_KEY:C_QKV] = _f32(cv_ref)

    row = lax.broadcasted_iota(jnp.int32, (CHUNK, CHUNK), 0)
    col = lax.broadcasted_iota(jnp.int32, (CHUNK, CHUNK), 1)
    eye = jnp.where(row == col, 1.0, 0.0).astype(F32)
    causal = row >= col
    merge_mask = []
    for lvl in range(6):
        bi, bj = row >> lvl, col >> lvl
        merge_mask.append((bi == bj + 1) & ((bi & 1) == 1))

    subl = lax.broadcasted_iota(jnp.int32, (SUBLANES, C_DK), 0)

    def conv_silu(c, lanes):
        r0 = hist + c * CHUNK
        n_piece = CHUNK // SUBLANES
        pieces = [xb_ref[r0 + (i - 1) * SUBLANES:r0 + i * SUBLANES, lanes] for i in range(n_piece + 1)]
        taps = [conv_ref[j:j + 1, lanes] for j in range(C_CONV)]
        rolled = {j: [pltpu.roll(p, j, 0) for p in pieces] for j in range(1, C_CONV)}
        outs = []
        for i in range(1, n_piece + 1):
            acc = pieces[i] * taps[C_CONV - 1]
            for j in range(1, C_CONV):
                acc = acc + jnp.where(subl < j, rolled[j][i - 1], rolled[j][i]) * taps[C_CONV - 1 - j]
            outs.append(acc)
        return _silu(jnp.concatenate(outs, axis=0))

    shared = []
    for c in range(tb // CHUNK):
        sm = sm_ref[c * CHUNK:(c + 1) * CHUNK, :]
        log2_a = (-LOG2_E) * jnp.exp(hp_ref[0:1, :]) * jax.nn.softplus(sm + hp_ref[1:2, :])
        b_all = _cumsum_rows(log2_a)
        shared.append(dict(
            beta=_sigmoid(sm), b=b_all, b_t=b_all.T, e=jnp.exp2(b_all),
            d=jnp.exp2(b_all[CHUNK - 1:CHUNK] - b_all)))

    def head(c, h):
        sh = shared[c]
        rows = slice(c * CHUNK, (c + 1) * CHUNK)
        hs = slice(h * C_DK, (h + 1) * C_DK)
        q = _l2norm_rows(conv_silu(c, hs)) * (C_DK ** -0.5)
        k = _l2norm_rows(conv_silu(c, slice(C_KEY + h * C_DK, C_KEY + (h + 1) * C_DK)))
        v = conv_silu(c, slice(2 * C_KEY + h * C_DV, 2 * C_KEY + (h + 1) * C_DV))
        b_col = sh["b"][:, SM_A + h:SM_A + h + 1]
        b_row = sh["b_t"][SM_A + h:SM_A + h + 1, :]
        beta = sh["beta"][:, SM_B + h:SM_B + h + 1]
        e_b = sh["e"][:, SM_A + h:SM_A + h + 1]
        decay = jnp.where(causal, jnp.exp2(jnp.minimum(b_col - b_row, 0.0)), 0.0)
        kb = k.astype(BF16)
        qb = q.astype(BF16)
        yield
        kk = _dot_nt(kb, kb)
        qk = _dot_nt(qb, kb)
        yield
        lower = beta * kk * decay
        qkd = (qk * decay).astype(BF16)
        x = eye - jnp.where(merge_mask[0], lower, 0.0)
        for lvl in range(1, 6):
            cpart = jnp.where(merge_mask[lvl], lower, 0.0).astype(BF16)
            xb = x.astype(BF16)
            y = _dot(cpart, xb).astype(BF16)
            yield
            x = x - _dot(xb, y)
            yield
        rhs = jnp.concatenate([v * beta, k * (beta * e_b)], axis=-1)
        sol = _dot(x.astype(BF16), rhs.astype(BF16))
        qe = (q * e_b).astype(BF16)
        kdec = (k * sh["d"][:, SM_A + h:SM_A + h + 1]).astype(BF16)
        return sol[:, :C_DV], sol[:, C_DV:].astype(BF16), qe, qkd, kdec

    def recur(c, h, u, w, qe, qkd, kdec):
        rows = slice(c * CHUNK, (c + 1) * CHUNK)
        hs = slice(h * C_DV, (h + 1) * C_DV)
        st = st_ref[h]
        stb = st.astype(BF16)
        yield
        v_new = (u - _dot_nt(w, stb)).astype(BF16)
        yield
        o = _dot_nt(qe, stb) + _dot(qkd, v_new)
        a_last = shared[c]["e"][CHUNK - 1:CHUNK, SM_A + h:SM_A + h + 1]
        st_ref[h] = a_last * st + _dot_tn(v_new, kdec)
        yield
        y_ref[rows, hs] = _head_out([o], onorm_ref[...], _f32(cg_ref, rows, hs),
                                    _f32(m_ref, rows, hs)).astype(BF16)

    n_chunk = tb // CHUNK
    pre = _run_interleaved([head(c, h) for c in range(n_chunk) for h in range(C_HEADS)], GDN_GROUP,
                           delay=[GDN_SKEW * c for c in range(n_chunk) for _ in range(C_HEADS)])
    xb_ref[0:hist, :] = xb_ref[tb:tb + hist, :]
    for c in range(n_chunk):
        _run_interleaved([recur(c, h, *pre[c * C_HEADS + h]) for h in range(C_HEADS)], C_HEADS)


GDN_TB = 256
GDN_GROUP = 32
GDN_SKEW = 0


def _mixer_c(proj3, small3, conv, head_params, onorm):
    tb = min(GDN_TB, proj3.shape[1])
    return _mixer_call(
        _mixer_c_kernel, "gated_deltanet_mixer", proj3,
        [BLK_CQ, BLK_CK, BLK_CV, BLK_CG, BLK_M2],
        [small3, conv, head_params, onorm],
        [pl.BlockSpec((None, tb, LANES), lambda b, t: (b, t, 0)),
         _full_spec((C_CONV, C_QKV)), _full_spec((SUBLANES, LANES)), _full_spec((1, C_DV))],
        [pltpu.VMEM((C_HEADS, C_DV, C_DK), F32), pltpu.VMEM((SUBLANES + tb, C_QKV), F32)],
        tb=tb)


MLP_TM = 256


def _out_mlp_kernel(x_ref, ya_ref, yb_ref, yc_ref, wo_ref, wu_ref, wd_ref, ln_ref, o_ref):
    y = (_f32(ya_ref) + _f32(yb_ref) + _f32(yc_ref)).astype(BF16)
    mix = _dot(y, wo_ref[...])
    x1 = x_ref[...] + _rms_rows(mix, ln_ref[0:1, :])
    h = _rms_rows(x1, ln_ref[1:2, :]).astype(BF16)
    up = _dot(h, wu_ref[...])
    act = jnp.square(jnp.maximum(up, 0.0)).astype(BF16)
    down = _dot(act, wd_ref[...])
    o_ref[...] = x1 + _rms_rows(down, ln_ref[2:3, :])


def _out_mlp(x2, ya, yb, yc, wo, wu, wd, ln):
    n = x2.shape[0]
    tm = min(MLP_TM, n)
    tok = pl.BlockSpec((tm, D_MODEL), lambda i: (i, 0))

    def resident(shape):
        return pl.BlockSpec(shape, lambda i: (0, 0), pipeline_mode=pl.Buffered(1))

    return pl.pallas_call(
        _out_mlp_kernel,
        grid=(n // tm,),
        in_specs=[tok, tok, tok, tok,
                  resident((D_MODEL, D_MODEL)), resident((D_MODEL, D_FF)), resident((D_FF, D_MODEL)),
                  resident((SUBLANES, D_MODEL))],
        out_specs=tok,
        out_shape=jax.ShapeDtypeStruct((n, D_MODEL), F32),
        compiler_params=pltpu.CompilerParams(
            dimension_semantics=("arbitrary",), vmem_limit_bytes=VMEM_LIMIT),
        name="merge_out_proj_mlp",
    )(x2, ya, yb, yc, wo, wu, wd, ln)


def _reorder_w_in(w_in):
    o_bgk = 4 * D_MODEL + 2 * B_KEY + D_MODEL
    o_bg = o_bgk + B_RANK
    o_cqkv = o_bg + D_MODEL
    o_ca = o_cqkv + C_QKV
    o_cg = o_ca + 2 * C_HEADS
    main = jnp.concatenate([w_in[..., :D_MODEL], w_in[..., 2 * D_MODEL:o_bgk], w_in[..., o_bg:o_ca],
                            w_in[..., o_cg:]], axis=-1)
    forget = w_in[..., D_MODEL:2 * D_MODEL]
    pad = jnp.zeros(w_in.shape[:-1] + (LANES - B_RANK - 2 * C_HEADS,), w_in.dtype)
    small = jnp.concatenate([w_in[..., o_bgk:o_bg], w_in[..., o_ca:o_cg], pad], axis=-1)
    return main.astype(BF16), forget.astype(BF16), small.astype(F32)


def kernel(x, ln_mix_pre, ln_mix_post, ln_mlp_pre, ln_mlp_post, w_in, hgrn_lb_logits, gla_w_gk, gla_b_gk,
           gdn_conv, gdn_a_log, gdn_dt_bias, hgrn_onorm, gla_onorm, gdn_onorm, w_out, w_up, w_down):
    bsz, t_len, _ = x.shape
    n = bsz * t_len
    assert t_len % CHUNK == 0 and n % min(PROJ_TM, n) == 0 and n % min(MLP_TM, n) == 0

    lbp = _lb_params(hgrn_lb_logits)
    w_main, w_forget, w_small = _reorder_w_in(w_in)
    wgk = jnp.pad(gla_w_gk.astype(F32), ((0, 0), (0, LANES - B_RANK), (0, 0)))
    lane_pad = ((0, 0), (SM_A, LANES - SM_A - C_HEADS))
    head_params = jnp.stack([jnp.pad(gdn_a_log.astype(F32), lane_pad),
                             jnp.pad(gdn_dt_bias.astype(F32), lane_pad)], axis=1)
    head_params = jnp.pad(head_params, ((0, 0), (0, SUBLANES - 2), (0, 0)))
    ln_rest = jnp.stack([ln_mix_post, ln_mlp_pre, ln_mlp_post], axis=1).astype(F32)
    ln_rest = jnp.pad(ln_rest, ((0, 0), (0, SUBLANES - 3), (0, 0)))
    wo, wu, wd = w_out.astype(BF16), w_up.astype(BF16), w_down.astype(BF16)

    x2 = x.reshape(n, D_MODEL).astype(F32)
    for layer in range(DEPTH):
        proj, forget, small = _proj(x2, ln_mix_pre[layer][None, :].astype(F32), w_main[layer],
                                    w_forget[layer], w_small[layer])
        proj3 = proj.reshape(bsz, t_len, N_MAIN)
        small3 = small.reshape(bsz, t_len, LANES)
        ya = _mixer_a(proj3, forget.reshape(bsz, t_len, D_MODEL), lbp[layer],
                      hgrn_onorm[layer][None, :].astype(F32))
        yb = _mixer_b(proj3, small3, wgk[layer], gla_b_gk[layer][None, :].astype(F32),
                      gla_onorm[layer][None, :].astype(F32))
        yc = _mixer_c(proj3, small3, gdn_conv[layer].astype(F32), head_params[layer],
                      gdn_onorm[layer][None, :].astype(F32))
        x2 = _out_mlp(x2, ya.reshape(n, D_MODEL), yb.reshape(n, D_MODEL), yc.reshape(n, D_MODEL),
                      wo[layer], wu[layer], wd[layer], ln_rest[layer])
    return x2.reshape(bsz, t_len, D_MODEL).astype(x.dtype)
```

```python
import functools

import jax
import jax.numpy as jnp
from jax import lax
from jax.experimental import pallas as pl
from jax.experimental.pallas import tpu as pltpu

F32 = jnp.float32
BF16 = jnp.bfloat16

D_MODEL = 1024
DEPTH = 4
CHUNK = 64
SUB = 16
N_SUB = CHUNK // SUB
EPS = 1e-6
LOG2_E = 1.4426950408889634
LANES = 128
SUBLANES = 8

A_HEADS, A_DK, A_DV = 8, 128, 128
B_HEADS, B_DK, B_DV = 4, 128, 256
B_KEY = B_HEADS * B_DK
B_RANK = 16
B_GATE_NORM = 16.0
C_HEADS, C_DK, C_DV = 8, 128, 128
C_KEY = C_HEADS * C_DK
C_CONV = 4
C_QKV = 2 * C_KEY + D_MODEL
D_FF = 4 * D_MODEL

BLK_AQ, BLK_AI, BLK_AG = 0, 1, 2
BLK_BQK, BLK_BV, BLK_BG = 3, 4, 5
BLK_CQ, BLK_CK, BLK_CV, BLK_CG = 6, 7, 8, 9
BLK_M0, BLK_M1, BLK_M2 = 10, 11, 12
N_MAIN = 13 * D_MODEL
SM_GK, SM_A, SM_B = 0, B_RANK, B_RANK + C_HEADS

VMEM_LIMIT = 56 * 1024 * 1024


def _dot(a, b):
    return jnp.dot(a, b, preferred_element_type=F32)


def _dot_nt(a, b):
    return lax.dot_general(a, b, (((1,), (1,)), ((), ())), preferred_element_type=F32)


def _dot_tn(a, b):
    return lax.dot_general(a, b, (((0,), (0,)), ((), ())), preferred_element_type=F32)


def _split2(a):
    hi = a.astype(BF16)
    lo = (a - hi.astype(F32)).astype(BF16)
    return hi, lo


def _dot3(a, b):
    ah, al = _split2(a)
    bh, bl = _split2(b)
    return _dot(ah, bh) + _dot(ah, bl) + _dot(al, bh)


def _cumsum_rows(g):
    row = lax.broadcasted_iota(jnp.int32, (CHUNK, CHUNK), 0)
    col = lax.broadcasted_iota(jnp.int32, (CHUNK, CHUNK), 1)
    tril = jnp.where(row >= col, 1.0, 0.0).astype(BF16)
    g1 = g.astype(BF16)
    r1 = g - g1.astype(F32)
    g2 = r1.astype(BF16)
    g3 = (r1 - g2.astype(F32)).astype(BF16)
    return _dot(tril, g1) + _dot(tril, g2) + _dot(tril, g3)


def _sigmoid(x):
    return jax.nn.sigmoid(x)


def _silu(x):
    return x * jax.nn.sigmoid(x)


def _log1p_exp_neg_abs(x):
    return jnp.log(1.0 + jnp.exp(-jnp.abs(x)))


def _log_sigmoid(x):
    return jnp.minimum(x, 0.0) - _log1p_exp_neg_abs(x)


def _rms_rows(x, w):
    return x * lax.rsqrt(jnp.mean(x * x, axis=-1, keepdims=True) + EPS) * w


def _lb_kernel(logits_ref, out_ref):
    lg = logits_ref[...]
    e = jnp.exp(lg - jnp.max(lg, axis=0, keepdims=True))
    p = e / jnp.sum(e, axis=0, keepdims=True)
    cum = p[0:1]
    first = cum
    pad = jnp.zeros((SUBLANES - 3, lg.shape[1]), F32)
    for layer in range(DEPTH):
        if layer > 0:
            cum = cum + p[layer:layer + 1]
        lb = jnp.clip(cum - first, 0.0, 1.0)
        out_ref[layer] = jnp.concatenate([jnp.log(lb), jnp.log1p(-lb), 1.0 - lb, pad], axis=0)


def _lb_params(logits):
    return pl.pallas_call(
        _lb_kernel,
        out_shape=jax.ShapeDtypeStruct((DEPTH, SUBLANES, logits.shape[1]), F32),
        name="hgrn_lower_bounds",
    )(logits.astype(F32))


PROJ_TM = 1024
PROJ_TN = N_MAIN // 8


def _proj_kernel(x_ref, lnw_ref, w_ref, wf_ref, ws_ref, o_ref, of_ref, os_ref, h_ref):
    @pl.when(pl.program_id(1) == 0)
    def _():
        h = _rms_rows(x_ref[...], lnw_ref[...])
        hh, hl = _split2(h)
        h_ref[...] = hh
        wh, wl = _split2(ws_ref[...])
        os_ref[...] = _dot(hh, wh) + _dot(hh, wl) + _dot(hl, wh)
        of_ref[...] = _dot(hh, wf_ref[...])

    o_ref[...] = _dot(h_ref[...], w_ref[...]).astype(BF16)


def _proj(x2, lnw, w_main, w_forget, w_small):
    n = x2.shape[0]
    tm = min(PROJ_TM, n)

    def resident(shape):
        return pl.BlockSpec(shape, lambda i, j: (0, 0), pipeline_mode=pl.Buffered(1))

    return pl.pallas_call(
        _proj_kernel,
        grid=(n // tm, N_MAIN // PROJ_TN),
        in_specs=[
            pl.BlockSpec((tm, D_MODEL), lambda i, j: (i, 0)),
            resident((1, D_MODEL)),
            pl.BlockSpec((D_MODEL, PROJ_TN), lambda i, j: (0, j)),
            resident((D_MODEL, D_MODEL)),
            resident((D_MODEL, LANES)),
        ],
        out_specs=[
            pl.BlockSpec((tm, PROJ_TN), lambda i, j: (i, j)),
            pl.BlockSpec((tm, D_MODEL), lambda i, j: (i, 0)),
            pl.BlockSpec((tm, LANES), lambda i, j: (i, 0)),
        ],
        out_shape=[
            jax.ShapeDtypeStruct((n, N_MAIN), BF16),
            jax.ShapeDtypeStruct((n, D_MODEL), F32),
            jax.ShapeDtypeStruct((n, LANES), F32),
        ],
        scratch_shapes=[pltpu.VMEM((tm, D_MODEL), BF16)],
        compiler_params=pltpu.CompilerParams(
            dimension_semantics=("arbitrary", "arbitrary"), vmem_limit_bytes=VMEM_LIMIT),
        name="rmsnorm_in_proj",
    )(x2, lnw, w_main, w_forget, w_small)


def _chunk_masks():
    row = lax.broadcasted_iota(jnp.int32, (CHUNK, CHUNK), 0)
    col = lax.broadcasted_iota(jnp.int32, (CHUNK, CHUNK), 1)
    sibling = []
    for lvl in range(6):
        bi, bj = row >> lvl, col >> lvl
        sibling.append((bi == bj + 1) & ((bi & 1) == 1))
    return row == col, sibling


def _run_interleaved(gens, group, delay=None):
    results = [None] * len(gens)
    for lo in range(0, len(gens), group):
        live = list(enumerate(gens))[lo:lo + group]
        rnd = 0
        while live:
            still = []
            for i, gen in live:
                if delay is not None and rnd < delay[i]:
                    still.append((i, gen))
                    continue
                try:
                    next(gen)
                    still.append((i, gen))
                except StopIteration as stop:
                    results[i] = stop.value
            live = still
            rnd += 1
    return results


def _gla_head(q, k, v, g, st_ref, kb_ref, h, masks):
    eye, sibling = masks
    width = q.shape[1]
    g = g * LOG2_E
    b = _cumsum_rows(g)
    yield
    b_last = b[CHUNK - 1:CHUNK]
    qd = (q * jnp.exp2(b)).astype(BF16)
    kd = (k * jnp.exp2(b_last - b)).astype(BF16)
    s_decay = jnp.exp2(b_last)
    vb = v.astype(BF16)
    qb = q.astype(BF16)
    kb = k.astype(BF16)

    kb_ref[0] = b - g
    kb_ref[1] = b

    subl = lax.broadcasted_iota(jnp.int32, (SUBLANES, width), 0)

    def block_rows(idx, s, off):
        cache, pieces = {}, []

        def bcast(r):
            if r not in cache:
                cache[r] = jnp.broadcast_to(kb_ref[idx, pl.ds(r, 1), :], (SUBLANES, width))
            return cache[r]

        for base in range(0, CHUNK, SUBLANES):
            if s >= SUBLANES:
                pieces.append(bcast(s * (base // s) + off))
            else:
                n_blk = SUBLANES // s
                piece = bcast(base + (n_blk - 1) * s + off)
                for t in reversed(range(n_blk - 1)):
                    piece = jnp.where(subl < (t + 1) * s, bcast(base + t * s + off), piece)
                pieces.append(piece)
        return jnp.concatenate(pieces, axis=0)

    q_lv = [(q * jnp.exp2(g)).astype(BF16)]
    k_lv = [kb]
    odd = (subl & 1) == 1
    g_prev, g_next = [], []
    for base in range(0, CHUNK, SUBLANES):
        piece = g[base:base + SUBLANES]
        g_prev.append(jnp.where(odd, pltpu.roll(piece, 1, 0), 0.0))
        g_next.append(jnp.where(odd, 0.0, pltpu.roll(piece, SUBLANES - 1, 0)))
    q_lv.append((q * jnp.exp2(g + jnp.concatenate(g_prev, axis=0))).astype(BF16))
    k_lv.append((k * jnp.exp2(jnp.concatenate(g_next, axis=0))).astype(BF16))
    for lvl in range(2, 6):
        s = 1 << lvl
        q_lv.append((q * jnp.exp2(b - block_rows(0, s, 0))).astype(BF16))
        k_lv.append((k * jnp.exp2(block_rows(1, s, s - 1) - b)).astype(BF16))

    yield
    first = _dot_nt(jnp.concatenate([q_lv[0], qb], axis=0), kb)
    prods = [_dot_nt(q_lv[lvl], k_lv[lvl]) for lvl in range(1, 6)]
    st = st_ref[h]
    o_inter = _dot_nt(qd, st.astype(BF16))
    st_ref[h] = st * s_decay + _dot_tn(vb, kd)
    yield
    a = jnp.where(sibling[0], first[:CHUNK], jnp.where(eye, first[CHUNK:], 0.0))
    for lvl in range(1, 6):
        a = jnp.where(sibling[lvl], prods[lvl - 1], a)
    ab = a.astype(BF16)
    yield
    return o_inter + _dot(ab, vb)


def _head_out(outs, onorm, gate, merge):
    y = jnp.concatenate([_rms_rows(o, onorm) for o in outs], axis=-1)
    return y * _silu(gate) * _sigmoid(merge)


def _f32(ref, rows=slice(None), lanes=slice(None)):
    return ref[rows, lanes].astype(F32)


def _mixer_a_kernel(aq_ref, ai_ref, ag_ref, m_ref, af_ref, lbp_ref, onorm_ref, y_ref, st_ref, kb_ref):
    @pl.when(pl.program_id(1) == 0)
    def _():
        st_ref[...] = jnp.zeros_like(st_ref)

    masks = _chunk_masks()

    def head(c, h):
        rows = slice(c * CHUNK, (c + 1) * CHUNK)
        hs = slice(h * A_DK, (h + 1) * A_DK)
        z = af_ref[rows, hs]
        log_lb, log_1m_lb, one_m_lb = lbp_ref[0:1, hs], lbp_ref[1:2, hs], lbp_ref[2:3, hs]
        t = log_1m_lb + _log_sigmoid(z)
        g = jnp.maximum(log_lb, t) + _log1p_exp_neg_abs(log_lb - t)
        k = one_m_lb * _sigmoid(-z)
        q = _silu(_f32(aq_ref, rows, hs))
        o = yield from _gla_head(q, k, _f32(ai_ref, rows, hs), g, st_ref, kb_ref.at[c * A_HEADS + h], h,
                                 masks)
        yield
        y = _head_out([o], onorm_ref[...], _f32(ag_ref, rows, hs), _f32(m_ref, rows, hs))
        y_ref[rows, hs] = y.astype(BF16)

    _run_interleaved([head(c, h) for c in range(aq_ref.shape[0] // CHUNK) for h in range(A_HEADS)],
                     GLA_GROUP)


def _mixer_b_kernel(qk_ref, v_ref, bg_ref, m_ref, sm_ref, wgk_ref, bgk_ref, onorm_ref, y_ref, st_ref, kb_ref):
    @pl.when(pl.program_id(1) == 0)
    def _():
        st_ref[...] = jnp.zeros_like(st_ref)

    masks = _chunk_masks()

    def head(c, h):
        rows = slice(c * CHUNK, (c + 1) * CHUNK)
        hs = slice(h * B_DK, (h + 1) * B_DK)
        vs = slice(h * B_DV, (h + 1) * B_DV)
        q = _f32(qk_ref, rows, hs) * (B_DK ** -0.5)
        k = _f32(qk_ref, rows, slice(B_KEY + h * B_DK, B_KEY + (h + 1) * B_DK))
        smh, sml = _split2(sm_ref[rows, :])
        wh, wl = _split2(wgk_ref[:, hs])
        gk = _dot(smh, wh) + _dot(smh, wl) + _dot(sml, wh) + bgk_ref[:, hs]
        yield
        g = _log_sigmoid(gk) / B_GATE_NORM
        o = yield from _gla_head(q, k, _f32(v_ref, rows, vs), g, st_ref, kb_ref.at[c * B_HEADS + h], h,
                                 masks)
        yield
        y = _head_out([o], onorm_ref[...], _f32(bg_ref, rows, vs), _f32(m_ref, rows, vs))
        y_ref[rows, vs] = y.astype(BF16)

    _run_interleaved([head(c, h) for c in range(qk_ref.shape[0] // CHUNK) for h in range(B_HEADS)],
                     GLA_GROUP)


def _tok_index(b, t, blk):
    return (b, t, blk)


def _full_spec(shape):
    return pl.BlockSpec(shape, lambda b, t: (0,) * len(shape))


def _mixer_call(kernel, name, proj3, tok_blocks, extra_inputs, extra_specs, scratch, tb=CHUNK):
    bsz, t_len, _ = proj3.shape
    assert t_len % tb == 0 and tb % CHUNK == 0
    return pl.pallas_call(
        kernel,
        grid=(bsz, t_len // tb),
        in_specs=[pl.BlockSpec((None, tb, D_MODEL), functools.partial(_tok_index, blk=blk))
                  for blk in tok_blocks] + extra_specs,
        out_specs=pl.BlockSpec((None, tb, D_MODEL), lambda b, t: (b, t, 0)),
        out_shape=jax.ShapeDtypeStruct((bsz, t_len, D_MODEL), BF16),
        scratch_shapes=scratch,
        compiler_params=pltpu.CompilerParams(
            dimension_semantics=("arbitrary", "arbitrary"), vmem_limit_bytes=VMEM_LIMIT),
        name=name,
    )(*([proj3] * len(tok_blocks)), *extra_inputs)


GLA_TB = 256
GLA_GROUP = 16


def _mixer_a(proj3, forget3, lbp, onorm):
    tb = min(GLA_TB, proj3.shape[1])
    return _mixer_call(
        _mixer_a_kernel, "hgrn2_mixer", proj3,
        [BLK_AQ, BLK_AI, BLK_AG, BLK_M0],
        [forget3, lbp, onorm],
        [pl.BlockSpec((None, tb, D_MODEL), lambda b, t: (b, t, 0)),
         _full_spec((SUBLANES, D_MODEL)), _full_spec((1, A_DV))],
        [pltpu.VMEM((A_HEADS, A_DV, A_DK), F32),
         pltpu.VMEM((tb // CHUNK * A_HEADS, 2, CHUNK, A_DK), F32)],
        tb=tb)


def _mixer_b(proj3, small3, wgk, bgk, onorm):
    tb = min(GLA_TB, proj3.shape[1])
    return _mixer_call(
        _mixer_b_kernel, "gla_mixer", proj3,
        [BLK_BQK, BLK_BV, BLK_BG, BLK_M1],
        [small3, wgk, bgk, onorm],
        [pl.BlockSpec((None, tb, LANES), lambda b, t: (b, t, 0)),
         _full_spec((LANES, B_KEY)), _full_spec((1, B_KEY)), _full_spec((1, B_DV))],
        [pltpu.VMEM((B_HEADS, B_DV, B_DK), F32),
         pltpu.VMEM((tb // CHUNK * B_HEADS, 2, CHUNK, B_DK), F32)],
        tb=tb)


def _l2norm_rows(x):
    return x * lax.rsqrt(jnp.sum(x * x, axis=-1, keepdims=True) + EPS)


def _mixer_c_kernel(cq_ref, ck_ref, cv_ref, cg_ref, m_ref, sm_ref, conv_ref, hp_ref, onorm_ref,
                    y_ref, st_ref, xb_ref):
    hist = SUBLANES

    @pl.when(pl.program_id(1) == 0)
    def _():
        st_ref[...] = jnp.zeros_like(st_ref)
        xb_ref[0:hist, :] = jnp.zeros((hist, C_QKV), F32)

    tb = cq_ref.shape[0]
    xb_ref[hist:hist + tb, 0:C_KEY] = _f32(cq_ref)
    xb_ref[hist:hist + tb, C_KEY:2 * C_KEY] = _f32(ck_ref)
    xb_ref[hist:hist + tb, 2 * C_KEY:C_QKV] = _f32(cv_ref)

    row = lax.broadcasted_iota(jnp.int32, (CHUNK, CHUNK), 0)
    col = lax.broadcasted_iota(jnp.int32, (CHUNK, CHUNK), 1)
    eye = jnp.where(row == col, 1.0, 0.0).astype(F32)
    causal = row >= col
    merge_mask = []
    for lvl in range(6):
        bi, bj = row >> lvl, col >> lvl
        merge_mask.append((bi == bj + 1) & ((bi & 1) == 1))

    subl = lax.broadcasted_iota(jnp.int32, (SUBLANES, C_DK), 0)

    def conv_silu(c, lanes):
        r0 = hist + c * CHUNK
        n_piece = CHUNK // SUBLANES
        pieces = [xb_ref[r0 + (i - 1) * SUBLANES:r0 + i * SUBLANES, lanes] for i in range(n_piece + 1)]
        taps = [conv_ref[j:j + 1, lanes] for j in range(C_CONV)]
        rolled = {j: [pltpu.roll(p, j, 0) for p in pieces] for j in range(1, C_CONV)}
        outs = []
        for i in range(1, n_piece + 1):
            acc = pieces[i] * taps[C_CONV - 1]
            for j in range(1, C_CONV):
                acc = acc + jnp.where(subl < j, rolled[j][i - 1], rolled[j][i]) * taps[C_CONV - 1 - j]
            outs.append(acc)
        return _silu(jnp.concatenate(outs, axis=0))

    shared = []
    for c in range(tb // CHUNK):
        sm = sm_ref[c * CHUNK:(c + 1) * CHUNK, :]
        log2_a = (-LOG2_E) * jnp.exp(hp_ref[0:1, :]) * jax.nn.softplus(sm + hp_ref[1:2, :])
        b_all = _cumsum_rows(log2_a)
        shared.append(dict(
            beta=_sigmoid(sm), b=b_all, b_t=b_all.T, e=jnp.exp2(b_all),
            d=jnp.exp2(b_all[CHUNK - 1:CHUNK] - b_all)))

    def head(c, h):
        sh = shared[c]
        rows = slice(c * CHUNK, (c + 1) * CHUNK)
        hs = slice(h * C_DK, (h + 1) * C_DK)
        q = _l2norm_rows(conv_silu(c, hs)) * (C_DK ** -0.5)
        k = _l2norm_rows(conv_silu(c, slice(C_KEY + h * C_DK, C_KEY + (h + 1) * C_DK)))
        v = conv_silu(c, slice(2 * C_KEY + h * C_DV, 2 * C_KEY + (h + 1) * C_DV))
        b_col = sh["b"][:, SM_A + h:SM_A + h + 1]
        b_row = sh["b_t"][SM_A + h:SM_A + h + 1, :]
        beta = sh["beta"][:, SM_B + h:SM_B + h + 1]
        e_b = sh["e"][:, SM_A + h:SM_A + h + 1]
        decay = jnp.where(causal, jnp.exp2(jnp.minimum(b_col - b_row, 0.0)), 0.0)
        kb = k.astype(BF16)
        qb = q.astype(BF16)
        yield
        kk = _dot_nt(kb, kb)
        qk = _dot_nt(qb, kb)
        yield
        lower = beta * kk * decay
        qkd = (qk * decay).astype(BF16)
        x = eye - jnp.where(merge_mask[0], lower, 0.0)
        for lvl in range(1, 6):
            cpart = jnp.where(merge_mask[lvl], lower, 0.0).astype(BF16)
            xb = x.astype(BF16)
            y = _dot(cpart, xb).astype(BF16)
            yield
            x = x - _dot(xb, y)
            yield
        rhs = jnp.concatenate([v * beta, k * (beta * e_b)], axis=-1)
        sol = _dot(x.astype(BF16), rhs.astype(BF16))
        qe = (q * e_b).astype(BF16)
        kdec = (k * sh["d"][:, SM_A + h:SM_A + h + 1]).astype(BF16)
        return sol[:, :C_DV], sol[:, C_DV:].astype(BF16), qe, qkd, kdec

    def recur(c, h, u, w, qe, qkd, kdec):
        rows = slice(c * CHUNK, (c + 1) * CHUNK)
        hs = slice(h * C_DV, (h + 1) * C_DV)
        st = st_ref[h]
        stb = st.astype(BF16)
        yield
        v_new = (u - _dot_nt(w, stb)).astype(BF16)
        yield
        o = _dot_nt(qe, stb) + _dot(qkd, v_new)
        a_last = shared[c]["e"][CHUNK - 1:CHUNK, SM_A + h:SM_A + h + 1]
        st_ref[h] = a_last * st + _dot_tn(v_new, kdec)
        yield
        y_ref[rows, hs] = _head_out([o], onorm_ref[...], _f32(cg_ref, rows, hs),
                                    _f32(m_ref, rows, hs)).astype(BF16)

    n_chunk = tb // CHUNK
    pre = _run_interleaved([head(c, h) for c in range(n_chunk) for h in range(C_HEADS)], GDN_GROUP,
                           delay=[GDN_SKEW * c for c in range(n_chunk) for _ in range(C_HEADS)])
    xb_ref[0:hist, :] = xb_ref[tb:tb + hist, :]
    for c in range(n_chunk):
        _run_interleaved([recur(c, h, *pre[c * C_HEADS + h]) for h in range(C_HEADS)], C_HEADS)


GDN_TB = 256
GDN_GROUP = 32
GDN_SKEW = 0


def _mixer_c(proj3, small3, conv, head_params, onorm):
    tb = min(GDN_TB, proj3.shape[1])
    return _mixer_call(
        _mixer_c_kernel, "gated_deltanet_mixer", proj3,
        [BLK_CQ, BLK_CK, BLK_CV, BLK_CG, BLK_M2],
        [small3, conv, head_params, onorm],
        [pl.BlockSpec((None, tb, LANES), lambda b, t: (b, t, 0)),
         _full_spec((C_CONV, C_QKV)), _full_spec((SUBLANES, LANES)), _full_spec((1, C_DV))],
        [pltpu.VMEM((C_HEADS, C_DV, C_DK), F32), pltpu.VMEM((SUBLANES + tb, C_QKV), F32)],
        tb=tb)


MLP_TM = 512
MLP_FF_CHUNK = 1024


def _out_mlp_kernel(x_ref, ya_ref, yb_ref, yc_ref, wo_ref, wu_ref, wd_ref, ln_ref, o_ref):
    y = (_f32(ya_ref) + _f32(yb_ref) + _f32(yc_ref)).astype(BF16)
    mix = _dot(y, wo_ref[...])
    x1 = x_ref[...] + _rms_rows(mix, ln_ref[0:1, :])
    h = _rms_rows(x1, ln_ref[1:2, :]).astype(BF16)
    down = None
    for lo in range(0, D_FF, MLP_FF_CHUNK):
        up = _dot(h, wu_ref[:, lo:lo + MLP_FF_CHUNK])
        act = jnp.square(jnp.maximum(up, 0.0)).astype(BF16)
        part = _dot(act, wd_ref[lo:lo + MLP_FF_CHUNK, :])
        down = part if down is None else down + part
    o_ref[...] = x1 + _rms_rows(down, ln_ref[2:3, :])


def _out_mlp(x2, ya, yb, yc, wo, wu, wd, ln):
    n = x2.shape[0]
    tm = min(MLP_TM, n)
    tok = pl.BlockSpec((tm, D_MODEL), lambda i: (i, 0))

    def resident(shape):
        return pl.BlockSpec(shape, lambda i: (0, 0), pipeline_mode=pl.Buffered(1))

    return pl.pallas_call(
        _out_mlp_kernel,
        grid=(n // tm,),
        in_specs=[tok, tok, tok, tok,
                  resident((D_MODEL, D_MODEL)), resident((D_MODEL, D_FF)), resident((D_FF, D_MODEL)),
                  resident((SUBLANES, D_MODEL))],
        out_specs=tok,
        out_shape=jax.ShapeDtypeStruct((n, D_MODEL), F32),
        compiler_params=pltpu.CompilerParams(
            dimension_semantics=("arbitrary",), vmem_limit_bytes=VMEM_LIMIT),
        name="merge_out_proj_mlp",
    )(x2, ya, yb, yc, wo, wu, wd, ln)


def _reorder_w_in(w_in):
    o_bgk = 4 * D_MODEL + 2 * B_KEY + D_MODEL
    o_bg = o_bgk + B_RANK
    o_cqkv = o_bg + D_MODEL
    o_ca = o_cqkv + C_QKV
    o_cg = o_ca + 2 * C_HEADS
    main = jnp.concatenate([w_in[..., :D_MODEL], w_in[..., 2 * D_MODEL:o_bgk], w_in[..., o_bg:o_ca],
                            w_in[..., o_cg:]], axis=-1)
    forget = w_in[..., D_MODEL:2 * D_MODEL]
    pad = jnp.zeros(w_in.shape[:-1] + (LANES - B_RANK - 2 * C_HEADS,), w_in.dtype)
    small = jnp.concatenate([w_in[..., o_bgk:o_bg], w_in[..., o_ca:o_cg], pad], axis=-1)
    return main.astype(BF16), forget.astype(BF16), small.astype(F32)


def kernel(x, ln_mix_pre, ln_mix_post, ln_mlp_pre, ln_mlp_post, w_in, hgrn_lb_logits, gla_w_gk, gla_b_gk,
           gdn_conv, gdn_a_log, gdn_dt_bias, hgrn_onorm, gla_onorm, gdn_onorm, w_out, w_up, w_down):
    bsz, t_len, _ = x.shape
    n = bsz * t_len
    assert t_len % CHUNK == 0 and n % min(PROJ_TM, n) == 0 and n % min(MLP_TM, n) == 0

    lbp = _lb_params(hgrn_lb_logits)
    w_main, w_forget, w_small = _reorder_w_in(w_in)
    wgk = jnp.pad(gla_w_gk.astype(F32), ((0, 0), (0, LANES - B_RANK), (0, 0)))
    lane_pad = ((0, 0), (SM_A, LANES - SM_A - C_HEADS))
    head_params = jnp.stack([jnp.pad(gdn_a_log.astype(F32), lane_pad),
                             jnp.pad(gdn_dt_bias.astype(F32), lane_pad)], axis=1)
    head_params = jnp.pad(head_params, ((0, 0), (0, SUBLANES - 2), (0, 0)))
    ln_rest = jnp.stack([ln_mix_post, ln_mlp_pre, ln_mlp_post], axis=1).astype(F32)
    ln_rest = jnp.pad(ln_rest, ((0, 0), (0, SUBLANES - 3), (0, 0)))
    wo, wu, wd = w_out.astype(BF16), w_up.astype(BF16), w_down.astype(BF16)

    x2 = x.reshape(n, D_MODEL).astype(F32)
    for layer in range(DEPTH):
        proj, forget, small = _proj(x2, ln_mix_pre[layer][None, :].astype(F32), w_main[layer],
                                    w_forget[layer], w_small[layer])
        proj3 = proj.reshape(bsz, t_len, N_MAIN)
        small3 = small.reshape(bsz, t_len, LANES)
        ya = _mixer_a(proj3, forget.reshape(bsz, t_len, D_MODEL), lbp[layer],
                      hgrn_onorm[layer][None, :].astype(F32))
        yb = _mixer_b(proj3, small3, wgk[layer], gla_b_gk[layer][None, :].astype(F32),
                      gla_onorm[layer][None, :].astype(F32))
        yc = _mixer_c(proj3, small3, gdn_conv[layer].astype(F32), head_params[layer],
                      gdn_onorm[layer][None, :].astype(F32))
        x2 = _out_mlp(x2, ya.reshape(n, D_MODEL), yb.reshape(n, D_MODEL), yc.reshape(n, D_MODEL),
                      wo[layer], wu[layer], wd[layer], ln_rest[layer])
    return x2.reshape(bsz, t_len, D_MODEL).astype(x.dtype)
```

```python
import functools

import jax
import jax.numpy as jnp
from jax import lax
from jax.experimental import pallas as pl
from jax.experimental.pallas import tpu as pltpu

F32 = jnp.float32
BF16 = jnp.bfloat16

D_MODEL = 1024
DEPTH = 4
CHUNK = 64
SUB = 16
N_SUB = CHUNK // SUB
EPS = 1e-6
LOG2_E = 1.4426950408889634
LANES = 128
SUBLANES = 8

A_HEADS, A_DK, A_DV = 8, 128, 128
B_HEADS, B_DK, B_DV = 4, 128, 256
B_KEY = B_HEADS * B_DK
B_RANK = 16
B_GATE_NORM = 16.0
C_HEADS, C_DK, C_DV = 8, 128, 128
C_KEY = C_HEADS * C_DK
C_CONV = 4
C_QKV = 2 * C_KEY + D_MODEL
D_FF = 4 * D_MODEL

BLK_AQ, BLK_AI, BLK_AG = 0, 1, 2
BLK_BQK, BLK_BV, BLK_BG = 3, 4, 5
BLK_CQ, BLK_CK, BLK_CV, BLK_CG = 6, 7, 8, 9
BLK_M0, BLK_M1, BLK_M2 = 10, 11, 12
N_MAIN = 13 * D_MODEL
SM_GK, SM_A, SM_B = 0, B_RANK, B_RANK + C_HEADS

VMEM_LIMIT = 56 * 1024 * 1024


def _dot(a, b):
    return jnp.dot(a, b, preferred_element_type=F32)


def _dot_nt(a, b):
    return lax.dot_general(a, b, (((1,), (1,)), ((), ())), preferred_element_type=F32)


def _dot_tn(a, b):
    return lax.dot_general(a, b, (((0,), (0,)), ((), ())), preferred_element_type=F32)


def _split2(a):
    hi = a.astype(BF16)
    lo = (a - hi.astype(F32)).astype(BF16)
    return hi, lo


def _dot3(a, b):
    ah, al = _split2(a)
    bh, bl = _split2(b)
    return _dot(ah, bh) + _dot(ah, bl) + _dot(al, bh)


def _cumsum_rows(g):
    row = lax.broadcasted_iota(jnp.int32, (CHUNK, CHUNK), 0)
    col = lax.broadcasted_iota(jnp.int32, (CHUNK, CHUNK), 1)
    tril = jnp.where(row >= col, 1.0, 0.0).astype(BF16)
    g1 = g.astype(BF16)
    r1 = g - g1.astype(F32)
    g2 = r1.astype(BF16)
    g3 = (r1 - g2.astype(F32)).astype(BF16)
    return _dot(tril, g1) + _dot(tril, g2) + _dot(tril, g3)


def _sigmoid(x):
    return jax.nn.sigmoid(x)


def _silu(x):
    return x * jax.nn.sigmoid(x)


def _log1p_exp_neg_abs(x):
    return jnp.log(1.0 + jnp.exp(-jnp.abs(x)))


def _log_sigmoid(x):
    return jnp.minimum(x, 0.0) - _log1p_exp_neg_abs(x)


def _rms_rows(x, w):
    return x * lax.rsqrt(jnp.mean(x * x, axis=-1, keepdims=True) + EPS) * w


def _lb_kernel(logits_ref, out_ref):
    lg = logits_ref[...]
    e = jnp.exp(lg - jnp.max(lg, axis=0, keepdims=True))
    p = e / jnp.sum(e, axis=0, keepdims=True)
    cum = p[0:1]
    first = cum
    pad = jnp.zeros((SUBLANES - 3, lg.shape[1]), F32)
    for layer in range(DEPTH):
        if layer > 0:
            cum = cum + p[layer:layer + 1]
        lb = jnp.clip(cum - first, 0.0, 1.0)
        out_ref[layer] = jnp.concatenate([jnp.log(lb), jnp.log1p(-lb), 1.0 - lb, pad], axis=0)


def _lb_params(logits):
    return pl.pallas_call(
        _lb_kernel,
        out_shape=jax.ShapeDtypeStruct((DEPTH, SUBLANES, logits.shape[1]), F32),
        name="hgrn_lower_bounds",
    )(logits.astype(F32))


PROJ_TM = 512
PROJ_TN = N_MAIN // 8


def _proj_kernel(x_ref, lnw_ref, w_ref, wf_ref, ws_ref, o_ref, of_ref, os_ref, h_ref):
    j = pl.program_id(1)

    @pl.when(j == 0)
    def _():
        h = _rms_rows(x_ref[...], lnw_ref[...])
        hh, hl = _split2(h)
        h_ref[...] = hh
        wh, wl = _split2(ws_ref[...])
        os_ref[...] = _dot(hh, wh) + _dot(hh, wl) + _dot(hl, wh)
        of_ref[...] = _dot(hh, wf_ref[...])

    cols = pl.ds(pl.multiple_of(j * PROJ_TN, LANES), PROJ_TN)
    o_ref[...] = _dot(h_ref[...], w_ref[:, cols]).astype(BF16)


def _proj(x2, lnw, w_main, w_forget, w_small):
    n = x2.shape[0]
    tm = min(PROJ_TM, n)

    def resident(shape):
        return pl.BlockSpec(shape, lambda i, j: (0, 0), pipeline_mode=pl.Buffered(1))

    return pl.pallas_call(
        _proj_kernel,
        grid=(n // tm, N_MAIN // PROJ_TN),
        in_specs=[
            pl.BlockSpec((tm, D_MODEL), lambda i, j: (i, 0)),
            resident((1, D_MODEL)),
            resident((D_MODEL, N_MAIN)),
            resident((D_MODEL, D_MODEL)),
            resident((D_MODEL, LANES)),
        ],
        out_specs=[
            pl.BlockSpec((tm, PROJ_TN), lambda i, j: (i, j)),
            pl.BlockSpec((tm, D_MODEL), lambda i, j: (i, 0)),
            pl.BlockSpec((tm, LANES), lambda i, j: (i, 0)),
        ],
        out_shape=[
            jax.ShapeDtypeStruct((n, N_MAIN), BF16),
            jax.ShapeDtypeStruct((n, D_MODEL), F32),
            jax.ShapeDtypeStruct((n, LANES), F32),
        ],
        scratch_shapes=[pltpu.VMEM((tm, D_MODEL), BF16)],
        compiler_params=pltpu.CompilerParams(
            dimension_semantics=("arbitrary", "arbitrary"), vmem_limit_bytes=VMEM_LIMIT),
        name="rmsnorm_in_proj",
    )(x2, lnw, w_main, w_forget, w_small)


def _chunk_masks():
    row = lax.broadcasted_iota(jnp.int32, (CHUNK, CHUNK), 0)
    col = lax.broadcasted_iota(jnp.int32, (CHUNK, CHUNK), 1)
    sibling = []
    for lvl in range(6):
        bi, bj = row >> lvl, col >> lvl
        sibling.append((bi == bj + 1) & ((bi & 1) == 1))
    return row == col, sibling


def _run_interleaved(gens, group, delay=None):
    results = [None] * len(gens)
    for lo in range(0, len(gens), group):
        live = list(enumerate(gens))[lo:lo + group]
        rnd = 0
        while live:
            still = []
            for i, gen in live:
                if delay is not None and rnd < delay[i]:
                    still.append((i, gen))
                    continue
                try:
                    next(gen)
                    still.append((i, gen))
                except StopIteration as stop:
                    results[i] = stop.value
            live = still
            rnd += 1
    return results


def _gla_head(q, k, v, g, st_ref, kb_ref, h, masks):
    eye, sibling = masks
    width = q.shape[1]
    g = g * LOG2_E
    b = _cumsum_rows(g)
    yield
    b_last = b[CHUNK - 1:CHUNK]
    qd = (q * jnp.exp2(b)).astype(BF16)
    kd = (k * jnp.exp2(b_last - b)).astype(BF16)
    s_decay = jnp.exp2(b_last)
    vb = v.astype(BF16)
    qb = q.astype(BF16)
    kb = k.astype(BF16)

    kb_ref[0] = b - g
    kb_ref[1] = b

    subl = lax.broadcasted_iota(jnp.int32, (SUBLANES, width), 0)

    def block_rows(idx, s, off):
        cache, pieces = {}, []

        def bcast(r):
            if r not in cache:
                cache[r] = jnp.broadcast_to(kb_ref[idx, pl.ds(r, 1), :], (SUBLANES, width))
            return cache[r]

        for base in range(0, CHUNK, SUBLANES):
            if s >= SUBLANES:
                pieces.append(bcast(s * (base // s) + off))
            else:
                n_blk = SUBLANES // s
                piece = bcast(base + (n_blk - 1) * s + off)
                for t in reversed(range(n_blk - 1)):
                    piece = jnp.where(subl < (t + 1) * s, bcast(base + t * s + off), piece)
                pieces.append(piece)
        return jnp.concatenate(pieces, axis=0)

    q_lv = [(q * jnp.exp2(g)).astype(BF16)]
    k_lv = [kb]
    odd = (subl & 1) == 1
    g_prev, g_next = [], []
    for base in range(0, CHUNK, SUBLANES):
        piece = g[base:base + SUBLANES]
        g_prev.append(jnp.where(odd, pltpu.roll(piece, 1, 0), 0.0))
        g_next.append(jnp.where(odd, 0.0, pltpu.roll(piece, SUBLANES - 1, 0)))
    q_lv.append((q * jnp.exp2(g + jnp.concatenate(g_prev, axis=0))).astype(BF16))
    k_lv.append((k * jnp.exp2(jnp.concatenate(g_next, axis=0))).astype(BF16))
    for lvl in range(2, 6):
        s = 1 << lvl
        q_lv.append((q * jnp.exp2(b - block_rows(0, s, 0))).astype(BF16))
        k_lv.append((k * jnp.exp2(block_rows(1, s, s - 1) - b)).astype(BF16))

    yield
    first = _dot_nt(jnp.concatenate([q_lv[0], qb], axis=0), kb)
    prods = [_dot_nt(q_lv[lvl], k_lv[lvl]) for lvl in range(1, 6)]
    st = st_ref[h]
    o_inter = _dot_nt(qd, st.astype(BF16))
    st_ref[h] = st * s_decay + _dot_tn(vb, kd)
    yield
    a = jnp.where(sibling[0], first[:CHUNK], jnp.where(eye, first[CHUNK:], 0.0))
    for lvl in range(1, 6):
        a = jnp.where(sibling[lvl], prods[lvl - 1], a)
    ab = a.astype(BF16)
    yield
    return o_inter + _dot(ab, vb)


def _head_out(outs, onorm, gate, merge):
    y = jnp.concatenate([_rms_rows(o, onorm) for o in outs], axis=-1)
    return y * _silu(gate) * _sigmoid(merge)


def _f32(ref, rows=slice(None), lanes=slice(None)):
    return ref[rows, lanes].astype(F32)


def _mixer_a_kernel(aq_ref, ai_ref, ag_ref, m_ref, af_ref, lbp_ref, onorm_ref, y_ref, st_ref, kb_ref):
    @pl.when(pl.program_id(1) == 0)
    def _():
        st_ref[...] = jnp.zeros_like(st_ref)

    masks = _chunk_masks()

    def head(c, h):
        rows = slice(c * CHUNK, (c + 1) * CHUNK)
        hs = slice(h * A_DK, (h + 1) * A_DK)
        z = af_ref[rows, hs]
        log_lb, log_1m_lb, one_m_lb = lbp_ref[0:1, hs], lbp_ref[1:2, hs], lbp_ref[2:3, hs]
        t = log_1m_lb + _log_sigmoid(z)
        g = jnp.maximum(log_lb, t) + _log1p_exp_neg_abs(log_lb - t)
        k = one_m_lb * _sigmoid(-z)
        q = _silu(_f32(aq_ref, rows, hs))
        o = yield from _gla_head(q, k, _f32(ai_ref, rows, hs), g, st_ref, kb_ref.at[c * A_HEADS + h], h,
                                 masks)
        yield
        y = _head_out([o], onorm_ref[...], _f32(ag_ref, rows, hs), _f32(m_ref, rows, hs))
        y_ref[rows, hs] = y.astype(BF16)

    _run_interleaved([head(c, h) for c in range(aq_ref.shape[0] // CHUNK) for h in range(A_HEADS)],
                     GLA_GROUP)


def _mixer_b_kernel(qk_ref, v_ref, bg_ref, m_ref, sm_ref, wgk_ref, bgk_ref, onorm_ref, y_ref, st_ref, kb_ref):
    @pl.when(pl.program_id(1) == 0)
    def _():
        st_ref[...] = jnp.zeros_like(st_ref)

    masks = _chunk_masks()

    def head(c, h):
        rows = slice(c * CHUNK, (c + 1) * CHUNK)
        hs = slice(h * B_DK, (h + 1) * B_DK)
        vs = slice(h * B_DV, (h + 1) * B_DV)
        q = _f32(qk_ref, rows, hs) * (B_DK ** -0.5)
        k = _f32(qk_ref, rows, slice(B_KEY + h * B_DK, B_KEY + (h + 1) * B_DK))
        smh, sml = _split2(sm_ref[rows, :])
        wh, wl = _split2(wgk_ref[:, hs])
        gk = _dot(smh, wh) + _dot(smh, wl) + _dot(sml, wh) + bgk_ref[:, hs]
        yield
        g = _log_sigmoid(gk) / B_GATE_NORM
        o = yield from _gla_head(q, k, _f32(v_ref, rows, vs), g, st_ref, kb_ref.at[c * B_HEADS + h], h,
                                 masks)
        yield
        y = _head_out([o], onorm_ref[...], _f32(bg_ref, rows, vs), _f32(m_ref, rows, vs))
        y_ref[rows, vs] = y.astype(BF16)

    _run_interleaved([head(c, h) for c in range(qk_ref.shape[0] // CHUNK) for h in range(B_HEADS)],
                     GLA_GROUP)


def _tok_index(b, t, blk):
    return (b, t, blk)


def _full_spec(shape):
    return pl.BlockSpec(shape, lambda b, t: (0,) * len(shape))


def _mixer_call(kernel, name, proj3, tok_blocks, extra_inputs, extra_specs, scratch, tb=CHUNK):
    bsz, t_len, _ = proj3.shape
    assert t_len % tb == 0 and tb % CHUNK == 0
    return pl.pallas_call(
        kernel,
        grid=(bsz, t_len // tb),
        in_specs=[pl.BlockSpec((None, tb, D_MODEL), functools.partial(_tok_index, blk=blk))
                  for blk in tok_blocks] + extra_specs,
        out_specs=pl.BlockSpec((None, tb, D_MODEL), lambda b, t: (b, t, 0)),
        out_shape=jax.ShapeDtypeStruct((bsz, t_len, D_MODEL), BF16),
        scratch_shapes=scratch,
        compiler_params=pltpu.CompilerParams(
            dimension_semantics=("arbitrary", "arbitrary"), vmem_limit_bytes=VMEM_LIMIT),
        name=name,
    )(*([proj3] * len(tok_blocks)), *extra_inputs)


GLA_TB = 256
GLA_GROUP = 16


def _mixer_a(proj3, forget3, lbp, onorm):
    tb = min(GLA_TB, proj3.shape[1])
    return _mixer_call(
        _mixer_a_kernel, "hgrn2_mixer", proj3,
        [BLK_AQ, BLK_AI, BLK_AG, BLK_M0],
        [forget3, lbp, onorm],
        [pl.BlockSpec((None, tb, D_MODEL), lambda b, t: (b, t, 0)),
         _full_spec((SUBLANES, D_MODEL)), _full_spec((1, A_DV))],
        [pltpu.VMEM((A_HEADS, A_DV, A_DK), F32),
         pltpu.VMEM((tb // CHUNK * A_HEADS, 2, CHUNK, A_DK), F32)],
        tb=tb)


def _mixer_b(proj3, small3, wgk, bgk, onorm):
    tb = min(GLA_TB, proj3.shape[1])
    return _mixer_call(
        _mixer_b_kernel, "gla_mixer", proj3,
        [BLK_BQK, BLK_BV, BLK_BG, BLK_M1],
        [small3, wgk, bgk, onorm],
        [pl.BlockSpec((None, tb, LANES), lambda b, t: (b, t, 0)),
         _full_spec((LANES, B_KEY)), _full_spec((1, B_KEY)), _full_spec((1, B_DV))],
        [pltpu.VMEM((B_HEADS, B_DV, B_DK), F32),
         pltpu.VMEM((tb // CHUNK * B_HEADS, 2, CHUNK, B_DK), F32)],
        tb=tb)


def _l2norm_rows(x):
    return x * lax.rsqrt(jnp.sum(x * x, axis=-1, keepdims=True) + EPS)


def _mixer_c_kernel(cq_ref, ck_ref, cv_ref, cg_ref, m_ref, sm_ref, conv_ref, hp_ref, onorm_ref,
                    y_ref, st_ref, xb_ref):
    hist = SUBLANES

    @pl.when(pl.program_id(1) == 0)
    def _():
        st_ref[...] = jnp.zeros_like(st_ref)
        xb_ref[0:hist, :] = jnp.zeros((hist, C_QKV), F32)

    tb = cq_ref.shape[0]
    xb_ref[hist:hist + tb, 0:C_KEY] = _f32(cq_ref)
    xb_ref[hist:hist + tb, C_KEY:2 * C_KEY] = _f32(ck_ref)
    xb_ref[hist:hist + tb, 2 * C_KEY:C_QKV] = _f32(cv_ref)

    row = lax.broadcasted_iota(jnp.int32, (CHUNK, CHUNK), 0)
    col = lax.broadcasted_iota(jnp.int32, (CHUNK, CHUNK), 1)
    eye = jnp.where(row == col, 1.0, 0.0).astype(F32)
    causal = row >= col
    merge_mask = []
    for lvl in range(6):
        bi, bj = row >> lvl, col >> lvl
        merge_mask.append((bi == bj + 1) & ((bi & 1) == 1))

    subl = lax.broadcasted_iota(jnp.int32, (SUBLANES, C_DK), 0)

    def conv_silu(c, lanes):
        r0 = hist + c * CHUNK
        n_piece = CHUNK // SUBLANES
        pieces = [xb_ref[r0 + (i - 1) * SUBLANES:r0 + i * SUBLANES, lanes] for i in range(n_piece + 1)]
        taps = [conv_ref[j:j + 1, lanes] for j in range(C_CONV)]
        rolled = {j: [pltpu.roll(p, j, 0) for p in pieces] for j in range(1, C_CONV)}
        outs = []
        for i in range(1, n_piece + 1):
            acc = pieces[i] * taps[C_CONV - 1]
            for j in range(1, C_CONV):
                acc = acc + jnp.where(subl < j, rolled[j][i - 1], rolled[j][i]) * taps[C_CONV - 1 - j]
            outs.append(acc)
        return _silu(jnp.concatenate(outs, axis=0))

    shared = []
    for c in range(tb // CHUNK):
        sm = sm_ref[c * CHUNK:(c + 1) * CHUNK, :]
        log2_a = (-LOG2_E) * jnp.exp(hp_ref[0:1, :]) * jax.nn.softplus(sm + hp_ref[1:2, :])
        b_all = _cumsum_rows(log2_a)
        shared.append(dict(
            beta=_sigmoid(sm), b=b_all, b_t=b_all.T, e=jnp.exp2(b_all),
            d=jnp.exp2(b_all[CHUNK - 1:CHUNK] - b_all)))

    def head(c, h):
        sh = shared[c]
        rows = slice(c * CHUNK, (c + 1) * CHUNK)
        hs = slice(h * C_DK, (h + 1) * C_DK)
        q = _l2norm_rows(conv_silu(c, hs)) * (C_DK ** -0.5)
        k = _l2norm_rows(conv_silu(c, slice(C_KEY + h * C_DK, C_KEY + (h + 1) * C_DK)))
        v = conv_silu(c, slice(2 * C_KEY + h * C_DV, 2 * C_KEY + (h + 1) * C_DV))
        b_col = sh["b"][:, SM_A + h:SM_A + h + 1]
        b_row = sh["b_t"][SM_A + h:SM_A + h + 1, :]
        beta = sh["beta"][:, SM_B + h:SM_B + h + 1]
        e_b = sh["e"][:, SM_A + h:SM_A + h + 1]
        decay = jnp.where(causal, jnp.exp2(jnp.minimum(b_col - b_row, 0.0)), 0.0)
        kb = k.astype(BF16)
        qb = q.astype(BF16)
        yield
        kk = _dot_nt(kb, kb)
        qk = _dot_nt(qb, kb)
        yield
        lower = beta * kk * decay
        qkd = (qk * decay).astype(BF16)
        x = eye - jnp.where(merge_mask[0], lower, 0.0)
        for lvl in range(1, 6):
            cpart = jnp.where(merge_mask[lvl], lower, 0.0).astype(BF16)
            xb = x.astype(BF16)
            y = _dot(cpart, xb).astype(BF16)
            yield
            x = x - _dot(xb, y)
            yield
        rhs = jnp.concatenate([v * beta, k * (beta * e_b)], axis=-1)
        sol = _dot(x.astype(BF16), rhs.astype(BF16))
        qe = (q * e_b).astype(BF16)
        kdec = (k * sh["d"][:, SM_A + h:SM_A + h + 1]).astype(BF16)
        return sol[:, :C_DV], sol[:, C_DV:].astype(BF16), qe, qkd, kdec

    def recur(c, h, u, w, qe, qkd, kdec):
        rows = slice(c * CHUNK, (c + 1) * CHUNK)
        hs = slice(h * C_DV, (h + 1) * C_DV)
        st = st_ref[h]
        stb = st.astype(BF16)
        yield
        v_new = (u - _dot_nt(w, stb)).astype(BF16)
        yield
        o = _dot_nt(qe, stb) + _dot(qkd, v_new)
        a_last = shared[c]["e"][CHUNK - 1:CHUNK, SM_A + h:SM_A + h + 1]
        st_ref[h] = a_last * st + _dot_tn(v_new, kdec)
        yield
        y_ref[rows, hs] = _head_out([o], onorm_ref[...], _f32(cg_ref, rows, hs),
                                    _f32(m_ref, rows, hs)).astype(BF16)

    n_chunk = tb // CHUNK
    pre = _run_interleaved([head(c, h) for c in range(n_chunk) for h in range(C_HEADS)], GDN_GROUP,
                           delay=[GDN_SKEW * c for c in range(n_chunk) for _ in range(C_HEADS)])
    xb_ref[0:hist, :] = xb_ref[tb:tb + hist, :]
    for c in range(n_chunk):
        _run_interleaved([recur(c, h, *pre[c * C_HEADS + h]) for h in range(C_HEADS)], C_HEADS)


GDN_TB = 256
GDN_GROUP = 32
GDN_SKEW = 0


def _mixer_c(proj3, small3, conv, head_params, onorm):
    tb = min(GDN_TB, proj3.shape[1])
    return _mixer_call(
        _mixer_c_kernel, "gated_deltanet_mixer", proj3,
        [BLK_CQ, BLK_CK, BLK_CV, BLK_CG, BLK_M2],
        [small3, conv, head_params, onorm],
        [pl.BlockSpec((None, tb, LANES), lambda b, t: (b, t, 0)),
         _full_spec((C_CONV, C_QKV)), _full_spec((SUBLANES, LANES)), _full_spec((1, C_DV))],
        [pltpu.VMEM((C_HEADS, C_DV, C_DK), F32), pltpu.VMEM((SUBLANES + tb, C_QKV), F32)],
        tb=tb)


MLP_TM = 512
MLP_FF_CHUNK = 1024


def _out_mlp_kernel(x_ref, ya_ref, yb_ref, yc_ref, wo_ref, wu_ref, wd_ref, ln_ref, o_ref):
    y = (_f32(ya_ref) + _f32(yb_ref) + _f32(yc_ref)).astype(BF16)
    mix = _dot(y, wo_ref[...])
    x1 = x_ref[...] + _rms_rows(mix, ln_ref[0:1, :])
    h = _rms_rows(x1, ln_ref[1:2, :]).astype(BF16)
    down = None
    for lo in range(0, D_FF, MLP_FF_CHUNK):
        up = _dot(h, wu_ref[:, lo:lo + MLP_FF_CHUNK])
        act = jnp.square(jnp.maximum(up, 0.0)).astype(BF16)
        part = _dot(act, wd_ref[lo:lo + MLP_FF_CHUNK, :])
        down = part if down is None else down + part
    o_ref[...] = x1 + _rms_rows(down, ln_ref[2:3, :])


def _out_mlp(x2, ya, yb, yc, wo, wu, wd, ln):
    n = x2.shape[0]
    tm = min(MLP_TM, n)
    tok = pl.BlockSpec((tm, D_MODEL), lambda i: (i, 0))

    def resident(shape):
        return pl.BlockSpec(shape, lambda i: (0, 0), pipeline_mode=pl.Buffered(1))

    return pl.pallas_call(
        _out_mlp_kernel,
        grid=(n // tm,),
        in_specs=[tok, tok, tok, tok,
                  resident((D_MODEL, D_MODEL)), resident((D_MODEL, D_FF)), resident((D_FF, D_MODEL)),
                  resident((SUBLANES, D_MODEL))],
        out_specs=tok,
        out_shape=jax.ShapeDtypeStruct((n, D_MODEL), F32),
        compiler_params=pltpu.CompilerParams(
            dimension_semantics=("arbitrary",), vmem_limit_bytes=VMEM_LIMIT),
        name="merge_out_proj_mlp",
    )(x2, ya, yb, yc, wo, wu, wd, ln)


def _reorder_w_in(w_in):
    o_bgk = 4 * D_MODEL + 2 * B_KEY + D_MODEL
    o_bg = o_bgk + B_RANK
    o_cqkv = o_bg + D_MODEL
    o_ca = o_cqkv + C_QKV
    o_cg = o_ca + 2 * C_HEADS
    main = jnp.concatenate([w_in[..., :D_MODEL], w_in[..., 2 * D_MODEL:o_bgk], w_in[..., o_bg:o_ca],
                            w_in[..., o_cg:]], axis=-1)
    forget = w_in[..., D_MODEL:2 * D_MODEL]
    pad = jnp.zeros(w_in.shape[:-1] + (LANES - B_RANK - 2 * C_HEADS,), w_in.dtype)
    small = jnp.concatenate([w_in[..., o_bgk:o_bg], w_in[..., o_ca:o_cg], pad], axis=-1)
    return main.astype(BF16), forget.astype(BF16), small.astype(F32)


def kernel(x, ln_mix_pre, ln_mix_post, ln_mlp_pre, ln_mlp_post, w_in, hgrn_lb_logits, gla_w_gk, gla_b_gk,
           gdn_conv, gdn_a_log, gdn_dt_bias, hgrn_onorm, gla_onorm, gdn_onorm, w_out, w_up, w_down):
    bsz, t_len, _ = x.shape
    n = bsz * t_len
    assert t_len % CHUNK == 0 and n % min(PROJ_TM, n) == 0 and n % min(MLP_TM, n) == 0

    lbp = _lb_params(hgrn_lb_logits)
    w_main, w_forget, w_small = _reorder_w_in(w_in)
    wgk = jnp.pad(gla_w_gk.astype(F32), ((0, 0), (0, LANES - B_RANK), (0, 0)))
    lane_pad = ((0, 0), (SM_A, LANES - SM_A - C_HEADS))
    head_params = jnp.stack([jnp.pad(gdn_a_log.astype(F32), lane_pad),
                             jnp.pad(gdn_dt_bias.astype(F32), lane_pad)], axis=1)
    head_params = jnp.pad(head_params, ((0, 0), (0, SUBLANES - 2), (0, 0)))
    ln_rest = jnp.stack([ln_mix_post, ln_mlp_pre, ln_mlp_post], axis=1).astype(F32)
    ln_rest = jnp.pad(ln_rest, ((0, 0), (0, SUBLANES - 3), (0, 0)))
    wo, wu, wd = w_out.astype(BF16), w_up.astype(BF16), w_down.astype(BF16)

    x2 = x.reshape(n, D_MODEL).astype(F32)
    for layer in range(DEPTH):
        proj, forget, small = _proj(x2, ln_mix_pre[layer][None, :].astype(F32), w_main[layer],
                                    w_forget[layer], w_small[layer])
        proj3 = proj.reshape(bsz, t_len, N_MAIN)
        small3 = small.reshape(bsz, t_len, LANES)
        ya = _mixer_a(proj3, forget.reshape(bsz, t_len, D_MODEL), lbp[layer],
                      hgrn_onorm[layer][None, :].astype(F32))
        yb = _mixer_b(proj3, small3, wgk[layer], gla_b_gk[layer][None, :].astype(F32),
                      gla_onorm[layer][None, :].astype(F32))
        yc = _mixer_c(proj3, small3, gdn_conv[layer].astype(F32), head_params[layer],
                      gdn_onorm[layer][None, :].astype(F32))
        x2 = _out_mlp(x2, ya.reshape(n, D_MODEL), yb.reshape(n, D_MODEL), yc.reshape(n, D_MODEL),
                      wo[layer], wu[layer], wd[layer], ln_rest[layer])
    return x2.reshape(bsz, t_len, D_MODEL).astype(x.dtype)
```

```python
import functools

import jax
import jax.numpy as jnp
from jax import lax
from jax.experimental import pallas as pl
from jax.experimental.pallas import tpu as pltpu

F32 = jnp.float32
BF16 = jnp.bfloat16

D_MODEL = 1024
DEPTH = 4
CHUNK = 64
CHUNK_LOG2 = 6
EPS = 1e-6
LOG2_E = 1.4426950408889634
LANES = 128
SUBLANES = 8

A_HEADS, A_DK, A_DV = 8, 128, 128
B_HEADS, B_DK, B_DV = 4, 128, 256
B_KEY = B_HEADS * B_DK
B_RANK = 16
B_GATE_NORM = 16.0
C_HEADS, C_DK, C_DV = 8, 128, 128
C_KEY = C_HEADS * C_DK
C_CONV = 4
C_QKV = 2 * C_KEY + D_MODEL
D_FF = 4 * D_MODEL

BLK_AQ, BLK_AI, BLK_AG = 0, 1, 2
BLK_BQK, BLK_BV, BLK_BG = 3, 4, 5
BLK_CQ, BLK_CK, BLK_CV, BLK_CG = 6, 7, 8, 9
BLK_M0, BLK_M1, BLK_M2 = 10, 11, 12
N_MAIN = 13 * D_MODEL
SM_GK, SM_A, SM_B = 0, B_RANK, B_RANK + C_HEADS

VMEM_LIMIT = 56 * 1024 * 1024


def _dot(a, b):
    return jnp.dot(a, b, preferred_element_type=F32)


def _dot_nt(a, b):
    return lax.dot_general(a, b, (((1,), (1,)), ((), ())), preferred_element_type=F32)


def _dot_tn(a, b):
    return lax.dot_general(a, b, (((0,), (0,)), ((), ())), preferred_element_type=F32)


def _split2(a):
    hi = a.astype(BF16)
    lo = (a - hi.astype(F32)).astype(BF16)
    return hi, lo


def _dot3(a, b):
    ah, al = _split2(a)
    bh, bl = _split2(b)
    return _dot(ah, bh) + _dot(ah, bl) + _dot(al, bh)


def _cumsum_rows(g):
    row = lax.broadcasted_iota(jnp.int32, (CHUNK, CHUNK), 0)
    col = lax.broadcasted_iota(jnp.int32, (CHUNK, CHUNK), 1)
    tril = jnp.where(row >= col, 1.0, 0.0).astype(BF16)
    g1 = g.astype(BF16)
    r1 = g - g1.astype(F32)
    g2 = r1.astype(BF16)
    g3 = (r1 - g2.astype(F32)).astype(BF16)
    return _dot(tril, g1) + _dot(tril, g2) + _dot(tril, g3)


def _sigmoid(x):
    return jax.nn.sigmoid(x)


def _silu(x):
    return x * jax.nn.sigmoid(x)


def _log1p_exp_neg_abs(x):
    return jnp.log(1.0 + jnp.exp(-jnp.abs(x)))


def _log_sigmoid(x):
    return jnp.minimum(x, 0.0) - _log1p_exp_neg_abs(x)


def _rms_rows(x, w):
    return x * lax.rsqrt(jnp.mean(x * x, axis=-1, keepdims=True) + EPS) * w


def _lb_kernel(logits_ref, out_ref):
    lg = logits_ref[...]
    e = jnp.exp(lg - jnp.max(lg, axis=0, keepdims=True))
    p = e / jnp.sum(e, axis=0, keepdims=True)
    cum = p[0:1]
    first = cum
    pad = jnp.zeros((SUBLANES - 3, lg.shape[1]), F32)
    for layer in range(DEPTH):
        if layer > 0:
            cum = cum + p[layer:layer + 1]
        lb = jnp.clip(cum - first, 0.0, 1.0)
        out_ref[layer] = jnp.concatenate([jnp.log(lb), jnp.log1p(-lb), 1.0 - lb, pad], axis=0)


def _lb_params(logits):
    return pl.pallas_call(
        _lb_kernel,
        out_shape=jax.ShapeDtypeStruct((DEPTH, SUBLANES, logits.shape[1]), F32),
        name="hgrn_lower_bounds",
    )(logits.astype(F32))


PROJ_TM = 1024
PROJ_TN = N_MAIN // 8


def _proj_kernel(x_ref, lnw_ref, w_ref, wf_ref, ws_ref, o_ref, of_ref, os_ref, h_ref):
    @pl.when(pl.program_id(1) == 0)
    def _():
        h = _rms_rows(x_ref[...], lnw_ref[...])
        hh, hl = _split2(h)
        h_ref[...] = hh
        wh, wl = _split2(ws_ref[...])
        os_ref[...] = _dot(hh, wh) + _dot(hh, wl) + _dot(hl, wh)
        of_ref[...] = _dot(hh, wf_ref[...])

    o_ref[...] = _dot(h_ref[...], w_ref[...]).astype(BF16)


def _proj(x2, lnw, w_main, w_forget, w_small):
    n = x2.shape[0]
    tm = min(PROJ_TM, n)

    def resident(shape):
        return pl.BlockSpec(shape, lambda i, j: (0, 0), pipeline_mode=pl.Buffered(1))

    return pl.pallas_call(
        _proj_kernel,
        grid=(n // tm, N_MAIN // PROJ_TN),
        in_specs=[
            pl.BlockSpec((tm, D_MODEL), lambda i, j: (i, 0)),
            resident((1, D_MODEL)),
            pl.BlockSpec((D_MODEL, PROJ_TN), lambda i, j: (0, j)),
            resident((D_MODEL, D_MODEL)),
            resident((D_MODEL, LANES)),
        ],
        out_specs=[
            pl.BlockSpec((tm, PROJ_TN), lambda i, j: (i, j)),
            pl.BlockSpec((tm, D_MODEL), lambda i, j: (i, 0)),
            pl.BlockSpec((tm, LANES), lambda i, j: (i, 0)),
        ],
        out_shape=[
            jax.ShapeDtypeStruct((n, N_MAIN), BF16),
            jax.ShapeDtypeStruct((n, D_MODEL), F32),
            jax.ShapeDtypeStruct((n, LANES), F32),
        ],
        scratch_shapes=[pltpu.VMEM((tm, D_MODEL), BF16)],
        compiler_params=pltpu.CompilerParams(
            dimension_semantics=("arbitrary", "arbitrary"), vmem_limit_bytes=VMEM_LIMIT),
        name="rmsnorm_in_proj",
    )(x2, lnw, w_main, w_forget, w_small)


def _chunk_masks():
    row = lax.broadcasted_iota(jnp.int32, (CHUNK, CHUNK), 0)
    col = lax.broadcasted_iota(jnp.int32, (CHUNK, CHUNK), 1)
    sibling = []
    for lvl in range(6):
        bi, bj = row >> lvl, col >> lvl
        sibling.append((bi == bj + 1) & ((bi & 1) == 1))
    return row == col, sibling


def _run_interleaved(gens, group, delay=None):
    results = [None] * len(gens)
    for lo in range(0, len(gens), group):
        live = list(enumerate(gens))[lo:lo + group]
        rnd = 0
        while live:
            still = []
            for i, gen in live:
                if delay is not None and rnd < delay[i]:
                    still.append((i, gen))
                    continue
                try:
                    next(gen)
                    still.append((i, gen))
                except StopIteration as stop:
                    results[i] = stop.value
            live = still
            rnd += 1
    return results


def _gla_head(q, k, v, g, st_ref, kb_ref, h, masks):
    eye, sibling = masks
    width = q.shape[1]
    g = g * LOG2_E
    b = _cumsum_rows(g)
    yield
    b_last = b[CHUNK - 1:CHUNK]
    qd = (q * jnp.exp2(b)).astype(BF16)
    kd = (k * jnp.exp2(b_last - b)).astype(BF16)
    s_decay = jnp.exp2(b_last)
    vb = v.astype(BF16)
    qb = q.astype(BF16)
    kb = k.astype(BF16)

    kb_ref[0] = b - g
    kb_ref[1] = b

    subl = lax.broadcasted_iota(jnp.int32, (SUBLANES, width), 0)

    def block_rows(idx, s, off):
        cache, pieces = {}, []

        def bcast(r):
            if r not in cache:
                cache[r] = jnp.broadcast_to(kb_ref[idx, pl.ds(r, 1), :], (SUBLANES, width))
            return cache[r]

        for base in range(0, CHUNK, SUBLANES):
            if s >= SUBLANES:
                pieces.append(bcast(s * (base // s) + off))
            else:
                n_blk = SUBLANES // s
                piece = bcast(base + (n_blk - 1) * s + off)
                for t in reversed(range(n_blk - 1)):
                    piece = jnp.where(subl < (t + 1) * s, bcast(base + t * s + off), piece)
                pieces.append(piece)
        return jnp.concatenate(pieces, axis=0)

    q_lv = [(q * jnp.exp2(g)).astype(BF16)]
    k_lv = [kb]
    odd = (subl & 1) == 1
    g_prev, g_next = [], []
    for base in range(0, CHUNK, SUBLANES):
        piece = g[base:base + SUBLANES]
        g_prev.append(jnp.where(odd, pltpu.roll(piece, 1, 0), 0.0))
        g_next.append(jnp.where(odd, 0.0, pltpu.roll(piece, SUBLANES - 1, 0)))
    q_lv.append((q * jnp.exp2(g + jnp.concatenate(g_prev, axis=0))).astype(BF16))
    k_lv.append((k * jnp.exp2(jnp.concatenate(g_next, axis=0))).astype(BF16))
    for lvl in range(2, 6):
        s = 1 << lvl
        q_lv.append((q * jnp.exp2(b - block_rows(0, s, 0))).astype(BF16))
        k_lv.append((k * jnp.exp2(block_rows(1, s, s - 1) - b)).astype(BF16))

    yield
    first = _dot_nt(jnp.concatenate([q_lv[0], qb], axis=0), kb)
    prods = [_dot_nt(q_lv[lvl], k_lv[lvl]) for lvl in range(1, 6)]
    st = st_ref[h]
    o_inter = _dot_nt(qd, st.astype(BF16))
    st_ref[h] = st * s_decay + _dot_tn(vb, kd)
    yield
    a = jnp.where(sibling[0], first[:CHUNK], jnp.where(eye, first[CHUNK:], 0.0))
    for lvl in range(1, 6):
        a = jnp.where(sibling[lvl], prods[lvl - 1], a)
    ab = a.astype(BF16)
    yield
    return o_inter + _dot(ab, vb)


def _head_out(outs, onorm, gate, merge):
    y = jnp.concatenate([_rms_rows(o, onorm) for o in outs], axis=-1)
    return y * _silu(gate) * _sigmoid(merge)


def _f32(ref, rows=slice(None), lanes=slice(None)):
    return ref[rows, lanes].astype(F32)


def _mixer_a_kernel(aq_ref, ai_ref, ag_ref, m_ref, af_ref, lbp_ref, onorm_ref, y_ref, st_ref, kb_ref):
    @pl.when(pl.program_id(1) == 0)
    def _():
        st_ref[...] = jnp.zeros_like(st_ref)

    masks = _chunk_masks()

    def head(c, h):
        rows = slice(c * CHUNK, (c + 1) * CHUNK)
        hs = slice(h * A_DK, (h + 1) * A_DK)
        z = af_ref[rows, hs]
        log_lb, log_1m_lb, one_m_lb = lbp_ref[0:1, hs], lbp_ref[1:2, hs], lbp_ref[2:3, hs]
        t = log_1m_lb + _log_sigmoid(z)
        g = jnp.maximum(log_lb, t) + _log1p_exp_neg_abs(log_lb - t)
        k = one_m_lb * _sigmoid(-z)
        q = _silu(_f32(aq_ref, rows, hs))
        o = yield from _gla_head(q, k, _f32(ai_ref, rows, hs), g, st_ref, kb_ref.at[c * A_HEADS + h], h,
                                 masks)
        yield
        y = _head_out([o], onorm_ref[...], _f32(ag_ref, rows, hs), _f32(m_ref, rows, hs))
        y_ref[rows, hs] = y.astype(BF16)

    _run_interleaved([head(c, h) for c in range(aq_ref.shape[0] // CHUNK) for h in range(A_HEADS)],
                     GLA_GROUP)


def _mixer_b_kernel(qk_ref, v_ref, bg_ref, m_ref, sm_ref, wgk_ref, bgk_ref, onorm_ref, y_ref, st_ref, kb_ref):
    @pl.when(pl.program_id(1) == 0)
    def _():
        st_ref[...] = jnp.zeros_like(st_ref)

    masks = _chunk_masks()

    def head(c, h):
        rows = slice(c * CHUNK, (c + 1) * CHUNK)
        hs = slice(h * B_DK, (h + 1) * B_DK)
        vs = slice(h * B_DV, (h + 1) * B_DV)
        q = _f32(qk_ref, rows, hs) * (B_DK ** -0.5)
        k = _f32(qk_ref, rows, slice(B_KEY + h * B_DK, B_KEY + (h + 1) * B_DK))
        smh, sml = _split2(sm_ref[rows, :])
        wh, wl = _split2(wgk_ref[:, hs])
        gk = _dot(smh, wh) + _dot(smh, wl) + _dot(sml, wh) + bgk_ref[:, hs]
        yield
        g = _log_sigmoid(gk) / B_GATE_NORM
        o = yield from _gla_head(q, k, _f32(v_ref, rows, vs), g, st_ref, kb_ref.at[c * B_HEADS + h], h,
                                 masks)
        yield
        y = _head_out([o], onorm_ref[...], _f32(bg_ref, rows, vs), _f32(m_ref, rows, vs))
        y_ref[rows, vs] = y.astype(BF16)

    _run_interleaved([head(c, h) for c in range(qk_ref.shape[0] // CHUNK) for h in range(B_HEADS)],
                     GLA_GROUP)


def _tok_index(b, t, blk):
    return (b, t, blk)


def _full_spec(shape):
    return pl.BlockSpec(shape, lambda b, t: (0,) * len(shape))


def _mixer_call(kernel, name, proj3, tok_blocks, extra_inputs, extra_specs, scratch, tb=CHUNK):
    bsz, t_len, _ = proj3.shape
    assert t_len % tb == 0 and tb % CHUNK == 0
    return pl.pallas_call(
        kernel,
        grid=(bsz, t_len // tb),
        in_specs=[pl.BlockSpec((None, tb, D_MODEL), functools.partial(_tok_index, blk=blk))
                  for blk in tok_blocks] + extra_specs,
        out_specs=pl.BlockSpec((None, tb, D_MODEL), lambda b, t: (b, t, 0)),
        out_shape=jax.ShapeDtypeStruct((bsz, t_len, D_MODEL), BF16),
        scratch_shapes=scratch,
        compiler_params=pltpu.CompilerParams(
            dimension_semantics=("arbitrary", "arbitrary"), vmem_limit_bytes=VMEM_LIMIT),
        name=name,
    )(*([proj3] * len(tok_blocks)), *extra_inputs)


GLA_TB = 256
GLA_GROUP = 16


def _mixer_a(proj3, forget3, lbp, onorm):
    tb = min(GLA_TB, proj3.shape[1])
    return _mixer_call(
        _mixer_a_kernel, "hgrn2_mixer", proj3,
        [BLK_AQ, BLK_AI, BLK_AG, BLK_M0],
        [forget3, lbp, onorm],
        [pl.BlockSpec((None, tb, D_MODEL), lambda b, t: (b, t, 0)),
         _full_spec((SUBLANES, D_MODEL)), _full_spec((1, A_DV))],
        [pltpu.VMEM((A_HEADS, A_DV, A_DK), F32),
         pltpu.VMEM((tb // CHUNK * A_HEADS, 2, CHUNK, A_DK), F32)],
        tb=tb)


def _mixer_b(proj3, small3, wgk, bgk, onorm):
    tb = min(GLA_TB, proj3.shape[1])
    return _mixer_call(
        _mixer_b_kernel, "gla_mixer", proj3,
        [BLK_BQK, BLK_BV, BLK_BG, BLK_M1],
        [small3, wgk, bgk, onorm],
        [pl.BlockSpec((None, tb, LANES), lambda b, t: (b, t, 0)),
         _full_spec((LANES, B_KEY)), _full_spec((1, B_KEY)), _full_spec((1, B_DV))],
        [pltpu.VMEM((B_HEADS, B_DV, B_DK), F32),
         pltpu.VMEM((tb // CHUNK * B_HEADS, 2, CHUNK, B_DK), F32)],
        tb=tb)


def _l2norm_rows(x):
    return x * lax.rsqrt(jnp.sum(x * x, axis=-1, keepdims=True) + EPS)


def _mixer_c_kernel(cq_ref, ck_ref, cv_ref, cg_ref, m_ref, sm_ref, conv_ref, hp_ref, onorm_ref,
                    y_ref, st_ref, xb_ref):
    hist = GDN_HIST

    @pl.when(pl.program_id(1) == 0)
    def _():
        st_ref[...] = jnp.zeros_like(st_ref)
        xb_ref[0:hist, :] = jnp.zeros((hist, C_QKV), xb_ref.dtype)

    tb = cq_ref.shape[0]
    xb_ref[hist:hist + tb, 0:C_KEY] = cq_ref[...]
    xb_ref[hist:hist + tb, C_KEY:2 * C_KEY] = ck_ref[...]
    xb_ref[hist:hist + tb, 2 * C_KEY:C_QKV] = cv_ref[...]

    row = lax.broadcasted_iota(jnp.int32, (CHUNK, CHUNK), 0)
    col = lax.broadcasted_iota(jnp.int32, (CHUNK, CHUNK), 1)
    eye = jnp.where(row == col, 1.0, 0.0).astype(F32)
    causal = row >= col
    merge_mask = []
    for lvl in range(6):
        bi, bj = row >> lvl, col >> lvl
        merge_mask.append((bi == bj + 1) & ((bi & 1) == 1))

    n_delay = C_CONV - 1
    dr = lax.broadcasted_iota(jnp.int32, (n_delay * CHUNK, hist + CHUNK), 0)
    dc = lax.broadcasted_iota(jnp.int32, (n_delay * CHUNK, hist + CHUNK), 1)
    delay_sel = jnp.where(dc == hist + (dr & (CHUNK - 1)) - ((dr >> CHUNK_LOG2) + 1), 1.0, 0.0)
    delay_sel = delay_sel.astype(xb_ref.dtype)

    def conv_silu(c, lanes):
        window = xb_ref[c * CHUNK:c * CHUNK + hist + CHUNK, lanes]
        delayed = _dot(delay_sel, window)
        acc = window[hist:].astype(F32) * conv_ref[C_CONV - 1:C_CONV, lanes]
        for j in range(1, C_CONV):
            acc = acc + delayed[(j - 1) * CHUNK:j * CHUNK] * conv_ref[C_CONV - 1 - j:C_CONV - j, lanes]
        return _silu(acc)

    shared = []
    for c in range(tb // CHUNK):
        sm = sm_ref[c * CHUNK:(c + 1) * CHUNK, :]
        log2_a = (-LOG2_E) * jnp.exp(hp_ref[0:1, :]) * jax.nn.softplus(sm + hp_ref[1:2, :])
        b_all = _cumsum_rows(log2_a)
        shared.append(dict(
            beta=_sigmoid(sm), b=b_all, b_t=b_all.T, e=jnp.exp2(b_all),
            d=jnp.exp2(b_all[CHUNK - 1:CHUNK] - b_all)))

    def head(c, h):
        sh = shared[c]
        rows = slice(c * CHUNK, (c + 1) * CHUNK)
        hs = slice(h * C_DK, (h + 1) * C_DK)
        q = _l2norm_rows(conv_silu(c, hs)) * (C_DK ** -0.5)
        k = _l2norm_rows(conv_silu(c, slice(C_KEY + h * C_DK, C_KEY + (h + 1) * C_DK)))
        v = conv_silu(c, slice(2 * C_KEY + h * C_DV, 2 * C_KEY + (h + 1) * C_DV))
        b_col = sh["b"][:, SM_A + h:SM_A + h + 1]
        b_row = sh["b_t"][SM_A + h:SM_A + h + 1, :]
        beta = sh["beta"][:, SM_B + h:SM_B + h + 1]
        e_b = sh["e"][:, SM_A + h:SM_A + h + 1]
        kb = k.astype(BF16)
        qb = q.astype(BF16)
        rhs = jnp.concatenate([v * beta, k * (beta * e_b)], axis=-1).astype(BF16)
        qe = (q * e_b).astype(BF16)
        kdec = (k * sh["d"][:, SM_A + h:SM_A + h + 1]).astype(BF16)
        yield
        decay = jnp.where(causal, jnp.exp2(jnp.minimum(b_col - b_row, 0.0)), 0.0)
        kk = _dot_nt(kb, kb)
        qk = _dot_nt(qb, kb)
        yield
        lower = beta * kk * decay
        qkd = (qk * decay).astype(BF16)
        x = eye - jnp.where(merge_mask[0], lower, 0.0)
        for lvl in range(1, 6):
            cpart = jnp.where(merge_mask[lvl], lower, 0.0).astype(BF16)
            xb = x.astype(BF16)
            y = _dot(cpart, xb).astype(BF16)
            yield
            x = x - _dot(xb, y)
            yield
        sol = _dot(x.astype(BF16), rhs)
        return sol[:, :C_DV], sol[:, C_DV:].astype(BF16), qe, qkd, kdec

    def recur(c, h, u, w, qe, qkd, kdec):
        rows = slice(c * CHUNK, (c + 1) * CHUNK)
        hs = slice(h * C_DV, (h + 1) * C_DV)
        st = st_ref[h]
        stb = st.astype(BF16)
        yield
        v_new = (u - _dot_nt(w, stb)).astype(BF16)
        yield
        o = _dot_nt(qe, stb) + _dot(qkd, v_new)
        a_last = shared[c]["e"][CHUNK - 1:CHUNK, SM_A + h:SM_A + h + 1]
        st_ref[h] = a_last * st + _dot_tn(v_new, kdec)
        yield
        y_ref[rows, hs] = _head_out([o], onorm_ref[...], _f32(cg_ref, rows, hs),
                                    _f32(m_ref, rows, hs)).astype(BF16)

    n_chunk = tb // CHUNK
    pre = _run_interleaved([head(c, h) for c in range(n_chunk) for h in range(C_HEADS)], GDN_GROUP,
                           delay=[GDN_SKEW * c for c in range(n_chunk) for _ in range(C_HEADS)])
    xb_ref[0:hist, :] = xb_ref[tb:tb + hist, :]
    for c in range(n_chunk):
        _run_interleaved([recur(c, h, *pre[c * C_HEADS + h]) for h in range(C_HEADS)], C_HEADS)


GDN_TB = 256
GDN_HIST = CHUNK
GDN_GROUP = 32
GDN_SKEW = 0


def _mixer_c(proj3, small3, conv, head_params, onorm):
    tb = min(GDN_TB, proj3.shape[1])
    return _mixer_call(
        _mixer_c_kernel, "gated_deltanet_mixer", proj3,
        [BLK_CQ, BLK_CK, BLK_CV, BLK_CG, BLK_M2],
        [small3, conv, head_params, onorm],
        [pl.BlockSpec((None, tb, LANES), lambda b, t: (b, t, 0)),
         _full_spec((C_CONV, C_QKV)), _full_spec((SUBLANES, LANES)), _full_spec((1, C_DV))],
        [pltpu.VMEM((C_HEADS, C_DV, C_DK), F32), pltpu.VMEM((GDN_HIST + tb, C_QKV), proj3.dtype)],
        tb=tb)


MLP_TM = 512
MLP_FF_CHUNK = 1024


def _out_mlp_kernel(x_ref, ya_ref, yb_ref, yc_ref, wo_ref, wu_ref, wd_ref, ln_ref, o_ref):
    y = (_f32(ya_ref) + _f32(yb_ref) + _f32(yc_ref)).astype(BF16)
    mix = _dot(y, wo_ref[...])
    x1 = x_ref[...] + _rms_rows(mix, ln_ref[0:1, :])
    h = _rms_rows(x1, ln_ref[1:2, :]).astype(BF16)
    down = None
    for lo in range(0, D_FF, MLP_FF_CHUNK):
        up = _dot(h, wu_ref[:, lo:lo + MLP_FF_CHUNK])
        act = jnp.square(jnp.maximum(up, 0.0)).astype(BF16)
        part = _dot(act, wd_ref[lo:lo + MLP_FF_CHUNK, :])
        down = part if down is None else down + part
    o_ref[...] = x1 + _rms_rows(down, ln_ref[2:3, :])


def _out_mlp(x2, ya, yb, yc, wo, wu, wd, ln):
    n = x2.shape[0]
    tm = min(MLP_TM, n)
    tok = pl.BlockSpec((tm, D_MODEL), lambda i: (i, 0))

    def resident(shape):
        return pl.BlockSpec(shape, lambda i: (0, 0), pipeline_mode=pl.Buffered(1))

    return pl.pallas_call(
        _out_mlp_kernel,
        grid=(n // tm,),
        in_specs=[tok, tok, tok, tok,
                  resident((D_MODEL, D_MODEL)), resident((D_MODEL, D_FF)), resident((D_FF, D_MODEL)),
                  resident((SUBLANES, D_MODEL))],
        out_specs=tok,
        out_shape=jax.ShapeDtypeStruct((n, D_MODEL), F32),
        compiler_params=pltpu.CompilerParams(
            dimension_semantics=("arbitrary",), vmem_limit_bytes=VMEM_LIMIT),
        name="merge_out_proj_mlp",
    )(x2, ya, yb, yc, wo, wu, wd, ln)


def _reorder_w_in(w_in):
    o_bgk = 4 * D_MODEL + 2 * B_KEY + D_MODEL
    o_bg = o_bgk + B_RANK
    o_cqkv = o_bg + D_MODEL
    o_ca = o_cqkv + C_QKV
    o_cg = o_ca + 2 * C_HEADS
    main = jnp.concatenate([w_in[..., :D_MODEL].astype(BF16), w_in[..., 2 * D_MODEL:o_bgk].astype(BF16),
                            w_in[..., o_bg:o_ca].astype(BF16), w_in[..., o_cg:].astype(BF16)], axis=-1)
    forget = w_in[..., D_MODEL:2 * D_MODEL]
    pad = jnp.zeros(w_in.shape[:-1] + (LANES - B_RANK - 2 * C_HEADS,), w_in.dtype)
    small = jnp.concatenate([w_in[..., o_bgk:o_bg], w_in[..., o_ca:o_cg], pad], axis=-1)
    return main, forget.astype(BF16), small.astype(F32)


def kernel(x, ln_mix_pre, ln_mix_post, ln_mlp_pre, ln_mlp_post, w_in, hgrn_lb_logits, gla_w_gk, gla_b_gk,
           gdn_conv, gdn_a_log, gdn_dt_bias, hgrn_onorm, gla_onorm, gdn_onorm, w_out, w_up, w_down):
    bsz, t_len, _ = x.shape
    n = bsz * t_len
    assert t_len % CHUNK == 0 and n % min(PROJ_TM, n) == 0 and n % min(MLP_TM, n) == 0

    lbp = _lb_params(hgrn_lb_logits)
    w_main, w_forget, w_small = _reorder_w_in(w_in)
    wgk = jnp.pad(gla_w_gk.astype(F32), ((0, 0), (0, LANES - B_RANK), (0, 0)))
    lane_pad = ((0, 0), (SM_A, LANES - SM_A - C_HEADS))
    head_params = jnp.stack([jnp.pad(gdn_a_log.astype(F32), lane_pad),
                             jnp.pad(gdn_dt_bias.astype(F32), lane_pad)], axis=1)
    head_params = jnp.pad(head_params, ((0, 0), (0, SUBLANES - 2), (0, 0)))
    ln_rest = jnp.stack([ln_mix_post, ln_mlp_pre, ln_mlp_post], axis=1).astype(F32)
    ln_rest = jnp.pad(ln_rest, ((0, 0), (0, SUBLANES - 3), (0, 0)))
    wo, wu, wd = w_out.astype(BF16), w_up.astype(BF16), w_down.astype(BF16)

    x2 = x.reshape(n, D_MODEL).astype(F32)
    for layer in range(DEPTH):
        proj, forget, small = _proj(x2, ln_mix_pre[layer][None, :].astype(F32), w_main[layer],
                                    w_forget[layer], w_small[layer])
        proj3 = proj.reshape(bsz, t_len, N_MAIN)
        small3 = small.reshape(bsz, t_len, LANES)
        ya = _mixer_a(proj3, forget.reshape(bsz, t_len, D_MODEL), lbp[layer],
                      hgrn_onorm[layer][None, :].astype(F32))
        yb = _mixer_b(proj3, small3, wgk[layer], gla_b_gk[layer][None, :].astype(F32),
                      gla_onorm[layer][None, :].astype(F32))
        yc = _mixer_c(proj3, small3, gdn_conv[layer].astype(F32), head_params[layer],
                      gdn_onorm[layer][None, :].astype(F32))
        x2 = _out_mlp(x2, ya.reshape(n, D_MODEL), yb.reshape(n, D_MODEL), yc.reshape(n, D_MODEL),
                      wo[layer], wu[layer], wd[layer], ln_rest[layer])
    return x2.reshape(bsz, t_len, D_MODEL).astype(x.dtype)
```

```python
import functools

import jax
import jax.numpy as jnp
from jax import lax
from jax.experimental import pallas as pl
from jax.experimental.pallas import tpu as pltpu

F32 = jnp.float32
BF16 = jnp.bfloat16

D_MODEL = 1024
DEPTH = 4
CHUNK = 64
CHUNK_LOG2 = 6
EPS = 1e-6
LOG2_E = 1.4426950408889634
LANES = 128
SUBLANES = 8

A_HEADS, A_DK, A_DV = 8, 128, 128
B_HEADS, B_DK, B_DV = 4, 128, 256
B_KEY = B_HEADS * B_DK
B_RANK = 16
B_GATE_NORM = 16.0
C_HEADS, C_DK, C_DV = 8, 128, 128
C_KEY = C_HEADS * C_DK
C_CONV = 4
C_QKV = 2 * C_KEY + D_MODEL
D_FF = 4 * D_MODEL

BLK_AQ, BLK_AI, BLK_AG = 0, 1, 2
BLK_BQK, BLK_BV, BLK_BG = 3, 4, 5
BLK_CQ, BLK_CK, BLK_CV, BLK_CG = 6, 7, 8, 9
BLK_M0, BLK_M1, BLK_M2 = 10, 11, 12
N_MAIN = 13 * D_MODEL
SM_GK, SM_A, SM_B = 0, B_RANK, B_RANK + C_HEADS

VMEM_LIMIT = 56 * 1024 * 1024


def _dot(a, b):
    return jnp.dot(a, b, preferred_element_type=F32)


def _dot_nt(a, b):
    return lax.dot_general(a, b, (((1,), (1,)), ((), ())), preferred_element_type=F32)


def _dot_tn(a, b):
    return lax.dot_general(a, b, (((0,), (0,)), ((), ())), preferred_element_type=F32)


def _split2(a):
    hi = a.astype(BF16)
    lo = (a - hi.astype(F32)).astype(BF16)
    return hi, lo


def _dot3(a, b):
    ah, al = _split2(a)
    bh, bl = _split2(b)
    return _dot(ah, bh) + _dot(ah, bl) + _dot(al, bh)


def _cumsum_rows(g):
    row = lax.broadcasted_iota(jnp.int32, (CHUNK, CHUNK), 0)
    col = lax.broadcasted_iota(jnp.int32, (CHUNK, CHUNK), 1)
    tril = jnp.where(row >= col, 1.0, 0.0).astype(BF16)
    g1 = g.astype(BF16)
    r1 = g - g1.astype(F32)
    g2 = r1.astype(BF16)
    g3 = (r1 - g2.astype(F32)).astype(BF16)
    return _dot(tril, g1) + _dot(tril, g2) + _dot(tril, g3)


def _sigmoid(x):
    return jax.nn.sigmoid(x)


def _silu(x):
    return x * jax.nn.sigmoid(x)


def _log1p_exp_neg_abs(x):
    return jnp.log(1.0 + jnp.exp(-jnp.abs(x)))


def _log_sigmoid(x):
    return jnp.minimum(x, 0.0) - _log1p_exp_neg_abs(x)


def _rms_rows(x, w):
    return x * lax.rsqrt(jnp.mean(x * x, axis=-1, keepdims=True) + EPS) * w


def _lb_kernel(logits_ref, out_ref):
    lg = logits_ref[...]
    e = jnp.exp(lg - jnp.max(lg, axis=0, keepdims=True))
    p = e / jnp.sum(e, axis=0, keepdims=True)
    cum = p[0:1]
    first = cum
    pad = jnp.zeros((SUBLANES - 3, lg.shape[1]), F32)
    for layer in range(DEPTH):
        if layer > 0:
            cum = cum + p[layer:layer + 1]
        lb = jnp.clip(cum - first, 0.0, 1.0)
        out_ref[layer] = jnp.concatenate([jnp.log(lb), jnp.log1p(-lb), 1.0 - lb, pad], axis=0)


def _lb_params(logits):
    return pl.pallas_call(
        _lb_kernel,
        out_shape=jax.ShapeDtypeStruct((DEPTH, SUBLANES, logits.shape[1]), F32),
        name="hgrn_lower_bounds",
    )(logits.astype(F32))


PROJ_TM = 1024
PROJ_TN = N_MAIN // 8


def _proj_kernel(x_ref, lnw_ref, w_ref, wf_ref, ws_ref, o_ref, of_ref, os_ref, h_ref):
    @pl.when(pl.program_id(1) == 0)
    def _():
        h = _rms_rows(x_ref[...], lnw_ref[...])
        hh, hl = _split2(h)
        h_ref[...] = hh
        wh, wl = _split2(ws_ref[...])
        os_ref[...] = _dot(hh, wh) + _dot(hh, wl) + _dot(hl, wh)
        of_ref[...] = _dot(hh, wf_ref[...])

    o_ref[...] = _dot(h_ref[...], w_ref[...]).astype(BF16)


def _proj(x2, lnw, w_main, w_forget, w_small):
    n = x2.shape[0]
    tm = min(PROJ_TM, n)

    def resident(shape):
        return pl.BlockSpec(shape, lambda i, j: (0, 0), pipeline_mode=pl.Buffered(1))

    return pl.pallas_call(
        _proj_kernel,
        grid=(n // tm, N_MAIN // PROJ_TN),
        in_specs=[
            pl.BlockSpec((tm, D_MODEL), lambda i, j: (i, 0)),
            resident((1, D_MODEL)),
            pl.BlockSpec((D_MODEL, PROJ_TN), lambda i, j: (0, j)),
            resident((D_MODEL, D_MODEL)),
            resident((D_MODEL, LANES)),
        ],
        out_specs=[
            pl.BlockSpec((tm, PROJ_TN), lambda i, j: (i, j)),
            pl.BlockSpec((tm, D_MODEL), lambda i, j: (i, 0)),
            pl.BlockSpec((tm, LANES), lambda i, j: (i, 0)),
        ],
        out_shape=[
            jax.ShapeDtypeStruct((n, N_MAIN), BF16),
            jax.ShapeDtypeStruct((n, D_MODEL), F32),
            jax.ShapeDtypeStruct((n, LANES), F32),
        ],
        scratch_shapes=[pltpu.VMEM((tm, D_MODEL), BF16)],
        compiler_params=pltpu.CompilerParams(
            dimension_semantics=("arbitrary", "arbitrary"), vmem_limit_bytes=VMEM_LIMIT),
        name="rmsnorm_in_proj",
    )(x2, lnw, w_main, w_forget, w_small)


def _chunk_masks():
    row = lax.broadcasted_iota(jnp.int32, (CHUNK, CHUNK), 0)
    col = lax.broadcasted_iota(jnp.int32, (CHUNK, CHUNK), 1)
    sibling = []
    for lvl in range(6):
        bi, bj = row >> lvl, col >> lvl
        sibling.append((bi == bj + 1) & ((bi & 1) == 1))
    return row == col, sibling


def _run_interleaved(gens, group, delay=None):
    results = [None] * len(gens)
    for lo in range(0, len(gens), group):
        live = list(enumerate(gens))[lo:lo + group]
        rnd = 0
        while live:
            still = []
            for i, gen in live:
                if delay is not None and rnd < delay[i]:
                    still.append((i, gen))
                    continue
                try:
                    next(gen)
                    still.append((i, gen))
                except StopIteration as stop:
                    results[i] = stop.value
            live = still
            rnd += 1
    return results


def _gla_head(q, k, v, g, st_ref, kb_ref, h, masks):
    eye, sibling = masks
    width = q.shape[1]
    g = g * LOG2_E
    b = _cumsum_rows(g)
    yield
    b_last = b[CHUNK - 1:CHUNK]
    qd = (q * jnp.exp2(b)).astype(BF16)
    kd = (k * jnp.exp2(b_last - b)).astype(BF16)
    s_decay = jnp.exp2(b_last)
    vb = v.astype(BF16)
    qb = q.astype(BF16)
    kb = k.astype(BF16)

    kb_ref[0] = b - g
    kb_ref[1] = b

    subl = lax.broadcasted_iota(jnp.int32, (SUBLANES, width), 0)

    def block_rows(idx, s, off):
        cache, pieces = {}, []

        def bcast(r):
            if r not in cache:
                cache[r] = jnp.broadcast_to(kb_ref[idx, pl.ds(r, 1), :], (SUBLANES, width))
            return cache[r]

        for base in range(0, CHUNK, SUBLANES):
            if s >= SUBLANES:
                pieces.append(bcast(s * (base // s) + off))
            else:
                n_blk = SUBLANES // s
                piece = bcast(base + (n_blk - 1) * s + off)
                for t in reversed(range(n_blk - 1)):
                    piece = jnp.where(subl < (t + 1) * s, bcast(base + t * s + off), piece)
                pieces.append(piece)
        return jnp.concatenate(pieces, axis=0)

    q_lv = [(q * jnp.exp2(g)).astype(BF16)]
    k_lv = [kb]
    odd = (subl & 1) == 1
    g_prev, g_next = [], []
    for base in range(0, CHUNK, SUBLANES):
        piece = g[base:base + SUBLANES]
        g_prev.append(jnp.where(odd, pltpu.roll(piece, 1, 0), 0.0))
        g_next.append(jnp.where(odd, 0.0, pltpu.roll(piece, SUBLANES - 1, 0)))
    q_lv.append((q * jnp.exp2(g + jnp.concatenate(g_prev, axis=0))).astype(BF16))
    k_lv.append((k * jnp.exp2(jnp.concatenate(g_next, axis=0))).astype(BF16))
    for lvl in range(2, 6):
        s = 1 << lvl
        q_lv.append((q * jnp.exp2(b - block_rows(0, s, 0))).astype(BF16))
        k_lv.append((k * jnp.exp2(block_rows(1, s, s - 1) - b)).astype(BF16))

    yield
    first = _dot_nt(jnp.concatenate([q_lv[0], qb], axis=0), kb)
    prods = [_dot_nt(q_lv[lvl], k_lv[lvl]) for lvl in range(1, 6)]
    st = st_ref[h]
    o_inter = _dot_nt(qd, st.astype(BF16))
    st_ref[h] = st * s_decay + _dot_tn(vb, kd)
    yield
    a = jnp.where(sibling[0], first[:CHUNK], jnp.where(eye, first[CHUNK:], 0.0))
    for lvl in range(1, 6):
        a = jnp.where(sibling[lvl], prods[lvl - 1], a)
    ab = a.astype(BF16)
    yield
    return o_inter + _dot(ab, vb)


def _head_out(outs, onorm, gate, merge):
    y = jnp.concatenate([_rms_rows(o, onorm) for o in outs], axis=-1)
    return y * _silu(gate) * _sigmoid(merge)


def _f32(ref, rows=slice(None), lanes=slice(None)):
    return ref[rows, lanes].astype(F32)


def _mixer_a_kernel(aq_ref, ai_ref, ag_ref, m_ref, af_ref, lbp_ref, onorm_ref, y_ref, st_ref, kb_ref):
    @pl.when(pl.program_id(1) == 0)
    def _():
        st_ref[...] = jnp.zeros_like(st_ref)

    masks = _chunk_masks()

    def head(c, h):
        rows = slice(c * CHUNK, (c + 1) * CHUNK)
        hs = slice(h * A_DK, (h + 1) * A_DK)
        z = af_ref[rows, hs]
        log_lb, log_1m_lb, one_m_lb = lbp_ref[0:1, hs], lbp_ref[1:2, hs], lbp_ref[2:3, hs]
        t = log_1m_lb + _log_sigmoid(z)
        g = jnp.maximum(log_lb, t) + _log1p_exp_neg_abs(log_lb - t)
        k = one_m_lb * _sigmoid(-z)
        q = _silu(_f32(aq_ref, rows, hs))
        o = yield from _gla_head(q, k, ai_ref[rows, hs], g, st_ref, kb_ref.at[c * A_HEADS + h], h, masks)
        yield
        y = _head_out([o], onorm_ref[...], _f32(ag_ref, rows, hs), _f32(m_ref, rows, hs))
        y_ref[rows, hs] = y.astype(BF16)

    _run_interleaved([head(c, h) for c in range(aq_ref.shape[0] // CHUNK) for h in range(A_HEADS)],
                     GLA_GROUP)


def _mixer_b_kernel(qk_ref, v_ref, bg_ref, m_ref, sm_ref, wgk_ref, bgk_ref, onorm_ref, y_ref, st_ref, kb_ref):
    @pl.when(pl.program_id(1) == 0)
    def _():
        st_ref[...] = jnp.zeros_like(st_ref)

    masks = _chunk_masks()

    def head(c, h):
        rows = slice(c * CHUNK, (c + 1) * CHUNK)
        hs = slice(h * B_DK, (h + 1) * B_DK)
        vs = slice(h * B_DV, (h + 1) * B_DV)
        q = _f32(qk_ref, rows, hs) * (B_DK ** -0.5)
        k = _f32(qk_ref, rows, slice(B_KEY + h * B_DK, B_KEY + (h + 1) * B_DK))
        smh, sml = _split2(sm_ref[rows, :])
        wh, wl = _split2(wgk_ref[:, hs])
        gk = _dot(smh, wh) + _dot(smh, wl) + _dot(sml, wh) + bgk_ref[:, hs]
        yield
        g = _log_sigmoid(gk) / B_GATE_NORM
        o = yield from _gla_head(q, k, v_ref[rows, vs], g, st_ref, kb_ref.at[c * B_HEADS + h], h, masks)
        yield
        y = _head_out([o], onorm_ref[...], _f32(bg_ref, rows, vs), _f32(m_ref, rows, vs))
        y_ref[rows, vs] = y.astype(BF16)

    _run_interleaved([head(c, h) for c in range(qk_ref.shape[0] // CHUNK) for h in range(B_HEADS)],
                     GLA_GROUP // 2)


def _tok_index(b, t, blk):
    return (b, t, blk)


def _full_spec(shape):
    return pl.BlockSpec(shape, lambda b, t: (0,) * len(shape))


def _mixer_call(kernel, name, proj3, tok_blocks, extra_inputs, extra_specs, scratch, tb=CHUNK):
    bsz, t_len, _ = proj3.shape
    assert t_len % tb == 0 and tb % CHUNK == 0
    return pl.pallas_call(
        kernel,
        grid=(bsz, t_len // tb),
        in_specs=[pl.BlockSpec((None, tb, D_MODEL), functools.partial(_tok_index, blk=blk))
                  for blk in tok_blocks] + extra_specs,
        out_specs=pl.BlockSpec((None, tb, D_MODEL), lambda b, t: (b, t, 0)),
        out_shape=jax.ShapeDtypeStruct((bsz, t_len, D_MODEL), BF16),
        scratch_shapes=scratch,
        compiler_params=pltpu.CompilerParams(
            dimension_semantics=("arbitrary", "arbitrary"), vmem_limit_bytes=VMEM_LIMIT),
        name=name,
    )(*([proj3] * len(tok_blocks)), *extra_inputs)


GLA_TB = 256
GLA_GROUP = 32


def _mixer_a(proj3, forget3, lbp, onorm):
    tb = min(GLA_TB, proj3.shape[1])
    return _mixer_call(
        _mixer_a_kernel, "hgrn2_mixer", proj3,
        [BLK_AQ, BLK_AI, BLK_AG, BLK_M0],
        [forget3, lbp, onorm],
        [pl.BlockSpec((None, tb, D_MODEL), lambda b, t: (b, t, 0)),
         _full_spec((SUBLANES, D_MODEL)), _full_spec((1, A_DV))],
        [pltpu.VMEM((A_HEADS, A_DV, A_DK), F32),
         pltpu.VMEM((tb // CHUNK * A_HEADS, 2, CHUNK, A_DK), F32)],
        tb=tb)


def _mixer_b(proj3, small3, wgk, bgk, onorm):
    tb = min(2 * GLA_TB, proj3.shape[1])
    return _mixer_call(
        _mixer_b_kernel, "gla_mixer", proj3,
        [BLK_BQK, BLK_BV, BLK_BG, BLK_M1],
        [small3, wgk, bgk, onorm],
        [pl.BlockSpec((None, tb, LANES), lambda b, t: (b, t, 0)),
         _full_spec((LANES, B_KEY)), _full_spec((1, B_KEY)), _full_spec((1, B_DV))],
        [pltpu.VMEM((B_HEADS, B_DV, B_DK), F32),
         pltpu.VMEM((tb // CHUNK * B_HEADS, 2, CHUNK, B_DK), F32)],
        tb=tb)


def _l2norm_rows(x):
    return x * lax.rsqrt(jnp.sum(x * x, axis=-1, keepdims=True) + EPS)


def _mixer_c_kernel(cq_ref, ck_ref, cv_ref, cg_ref, m_ref, sm_ref, conv_ref, hp_ref, onorm_ref,
                    y_ref, st_ref, xb_ref):
    hist = GDN_HIST

    @pl.when(pl.program_id(1) == 0)
    def _():
        st_ref[...] = jnp.zeros_like(st_ref)
        xb_ref[0:hist, :] = jnp.zeros((hist, C_QKV), xb_ref.dtype)

    tb = cq_ref.shape[0]
    xb_ref[hist:hist + tb, 0:C_KEY] = cq_ref[...]
    xb_ref[hist:hist + tb, C_KEY:2 * C_KEY] = ck_ref[...]
    xb_ref[hist:hist + tb, 2 * C_KEY:C_QKV] = cv_ref[...]

    row = lax.broadcasted_iota(jnp.int32, (CHUNK, 2 * CHUNK), 0)
    lane = lax.broadcasted_iota(jnp.int32, (CHUNK, 2 * CHUNK), 1)
    col = lane & (CHUNK - 1)
    even = lane < CHUNK
    eye = jnp.where(row == col, 1.0, 0.0).astype(F32)
    causal = row >= col
    merge_mask = []
    for lvl in range(6):
        bi, bj = row >> lvl, col >> lvl
        merge_mask.append((bi == bj + 1) & ((bi & 1) == 1))
    keep_even = jnp.where(even, 1.0, 0.0).astype(BF16)
    keep_odd = jnp.where(even, 0.0, 1.0).astype(BF16)

    def blockdiag(m):
        return jnp.concatenate([m * keep_even, m * keep_odd], axis=0)

    def blockdiag_wide(m0, m1):
        z0, z1 = jnp.zeros_like(m1), jnp.zeros_like(m0)
        return jnp.concatenate([jnp.concatenate([m0, z0], axis=1), jnp.concatenate([z1, m1], axis=1)], axis=0)

    n_delay = C_CONV - 1
    dr = lax.broadcasted_iota(jnp.int32, (n_delay * CHUNK, hist + CHUNK), 0)
    dc = lax.broadcasted_iota(jnp.int32, (n_delay * CHUNK, hist + CHUNK), 1)
    delay_sel = jnp.where(dc == hist + (dr & (CHUNK - 1)) - ((dr >> CHUNK_LOG2) + 1), 1.0, 0.0)
    delay_sel = delay_sel.astype(xb_ref.dtype)

    def conv_silu(c, lanes):
        window = xb_ref[c * CHUNK:c * CHUNK + hist + CHUNK, lanes]
        delayed = _dot(delay_sel, window)
        acc = window[hist:].astype(F32) * conv_ref[C_CONV - 1:C_CONV, lanes]
        for j in range(1, C_CONV):
            acc = acc + delayed[(j - 1) * CHUNK:j * CHUNK] * conv_ref[C_CONV - 1 - j:C_CONV - j, lanes]
        return _silu(acc)

    shared = []
    for c in range(tb // CHUNK):
        sm = sm_ref[c * CHUNK:(c + 1) * CHUNK, :]
        log2_a = (-LOG2_E) * jnp.exp(hp_ref[0:1, :]) * jax.nn.softplus(sm + hp_ref[1:2, :])
        b_all = _cumsum_rows(log2_a)
        shared.append(dict(
            beta=_sigmoid(sm), b=b_all, b_t=jnp.concatenate([b_all, b_all], axis=0).T, e=jnp.exp2(b_all),
            d=jnp.exp2(b_all[CHUNK - 1:CHUNK] - b_all)))

    def pair(c, p):
        sh = shared[c]
        heads = (2 * p, 2 * p + 1)
        q, k, rhs, qe, kdec, beta, b_col = [], [], [], [], [], [], []
        for h in heads:
            q_h = _l2norm_rows(conv_silu(c, slice(h * C_DK, (h + 1) * C_DK))) * (C_DK ** -0.5)
            k_h = _l2norm_rows(conv_silu(c, slice(C_KEY + h * C_DK, C_KEY + (h + 1) * C_DK)))
            v_h = conv_silu(c, slice(2 * C_KEY + h * C_DV, 2 * C_KEY + (h + 1) * C_DV))
            beta_h = sh["beta"][:, SM_B + h:SM_B + h + 1]
            e_b = sh["e"][:, SM_A + h:SM_A + h + 1]
            q.append(q_h.astype(BF16))
            k.append(k_h.astype(BF16))
            rhs.append(jnp.concatenate([v_h * beta_h, k_h * (beta_h * e_b)], axis=-1).astype(BF16))
            qe.append((q_h * e_b).astype(BF16))
            kdec.append((k_h * sh["d"][:, SM_A + h:SM_A + h + 1]).astype(BF16))
            beta.append(beta_h)
            b_col.append(sh["b"][:, SM_A + h:SM_A + h + 1])
        yield
        b_row = jnp.where(even[0:1], sh["b_t"][SM_A + heads[0]:SM_A + heads[0] + 1, :],
                          sh["b_t"][SM_A + heads[1]:SM_A + heads[1] + 1, :])
        decay = jnp.where(causal, jnp.exp2(jnp.minimum(jnp.where(even, b_col[0], b_col[1]) - b_row, 0.0)), 0.0)
        k_diag = blockdiag_wide(k[0], k[1])
        kk = _dot_nt(jnp.concatenate(k, axis=1), k_diag)
        qk = _dot_nt(jnp.concatenate(q, axis=1), k_diag)
        yield
        lower = jnp.where(even, beta[0], beta[1]) * kk * decay
        qkd = (qk * decay).astype(BF16)
        x = eye - jnp.where(merge_mask[0], lower, 0.0)
        for lvl in range(1, 6):
            cpart = jnp.where(merge_mask[lvl], lower, 0.0).astype(BF16)
            xb = x.astype(BF16)
            y = _dot(cpart, blockdiag(xb)).astype(BF16)
            yield
            x = x - _dot(xb, blockdiag(y))
            yield
        sol = _dot(x.astype(BF16), blockdiag_wide(rhs[0], rhs[1]))
        width = C_DV + C_DK
        u = [sol[:, i * width:i * width + C_DV] for i in range(2)]
        w = [sol[:, i * width + C_DV:(i + 1) * width].astype(BF16) for i in range(2)]
        return u, w, qe, qkd, kdec

    def recur(c, p, u, w, qe, qkd, kdec):
        rows = slice(c * CHUNK, (c + 1) * CHUNK)
        heads = (2 * p, 2 * p + 1)
        lanes = slice(heads[0] * C_DV, (heads[1] + 1) * C_DV)
        st = [st_ref[h] for h in heads]
        stb = [s.astype(BF16) for s in st]
        yield
        v_new = [(u[i] - _dot_nt(w[i], stb[i])).astype(BF16) for i in range(2)]
        yield
        o_intra = _dot(qkd, blockdiag_wide(v_new[0], v_new[1]))
        o = [_dot_nt(qe[i], stb[i]) + o_intra[:, i * C_DV:(i + 1) * C_DV] for i in range(2)]
        for i, h in enumerate(heads):
            a_last = shared[c]["e"][CHUNK - 1:CHUNK, SM_A + h:SM_A + h + 1]
            st_ref[h] = a_last * st[i] + _dot_tn(v_new[i], kdec[i])
        yield
        y_ref[rows, lanes] = _head_out(o, onorm_ref[...], _f32(cg_ref, rows, lanes),
                                       _f32(m_ref, rows, lanes)).astype(BF16)

    n_chunk, n_pair = tb // CHUNK, C_HEADS // 2
    pre = _run_interleaved([pair(c, p) for c in range(n_chunk) for p in range(n_pair)], GDN_GROUP)
    xb_ref[0:hist, :] = xb_ref[tb:tb + hist, :]
    for c in range(n_chunk):
        _run_interleaved([recur(c, p, *pre[c * n_pair + p]) for p in range(n_pair)], n_pair)


GDN_TB = 512
GDN_HIST = CHUNK
GDN_GROUP = 32


def _mixer_c(proj3, small3, conv, head_params, onorm):
    tb = min(GDN_TB, proj3.shape[1])
    return _mixer_call(
        _mixer_c_kernel, "gated_deltanet_mixer", proj3,
        [BLK_CQ, BLK_CK, BLK_CV, BLK_CG, BLK_M2],
        [small3, conv, head_params, onorm],
        [pl.BlockSpec((None, tb, LANES), lambda b, t: (b, t, 0)),
         _full_spec((C_CONV, C_QKV)), _full_spec((SUBLANES, LANES)), _full_spec((1, C_DV))],
        [pltpu.VMEM((C_HEADS, C_DV, C_DK), F32), pltpu.VMEM((GDN_HIST + tb, C_QKV), proj3.dtype)],
        tb=tb)


MLP_TM = 512
MLP_FF_CHUNK = 1024


def _out_mlp_kernel(x_ref, ya_ref, yb_ref, yc_ref, wo_ref, wu_ref, wd_ref, ln_ref, o_ref):
    y = (_f32(ya_ref) + _f32(yb_ref) + _f32(yc_ref)).astype(BF16)
    mix = _dot(y, wo_ref[...])
    x1 = x_ref[...] + _rms_rows(mix, ln_ref[0:1, :])
    h = _rms_rows(x1, ln_ref[1:2, :]).astype(BF16)
    down = None
    for lo in range(0, D_FF, MLP_FF_CHUNK):
        up = _dot(h, wu_ref[:, lo:lo + MLP_FF_CHUNK])
        act = jnp.square(jnp.maximum(up, 0.0)).astype(BF16)
        part = _dot(act, wd_ref[lo:lo + MLP_FF_CHUNK, :])
        down = part if down is None else down + part
    o_ref[...] = x1 + _rms_rows(down, ln_ref[2:3, :])


def _out_mlp(x2, ya, yb, yc, wo, wu, wd, ln):
    n = x2.shape[0]
    tm = min(MLP_TM, n)
    tok = pl.BlockSpec((tm, D_MODEL), lambda i: (i, 0))

    def resident(shape):
        return pl.BlockSpec(shape, lambda i: (0, 0), pipeline_mode=pl.Buffered(1))

    return pl.pallas_call(
        _out_mlp_kernel,
        grid=(n // tm,),
        in_specs=[tok, tok, tok, tok,
                  resident((D_MODEL, D_MODEL)), resident((D_MODEL, D_FF)), resident((D_FF, D_MODEL)),
                  resident((SUBLANES, D_MODEL))],
        out_specs=tok,
        out_shape=jax.ShapeDtypeStruct((n, D_MODEL), F32),
        compiler_params=pltpu.CompilerParams(
            dimension_semantics=("arbitrary",), vmem_limit_bytes=VMEM_LIMIT),
        name="merge_out_proj_mlp",
    )(x2, ya, yb, yc, wo, wu, wd, ln)


def _reorder_w_in(w_in):
    o_bgk = 4 * D_MODEL + 2 * B_KEY + D_MODEL
    o_bg = o_bgk + B_RANK
    o_cqkv = o_bg + D_MODEL
    o_ca = o_cqkv + C_QKV
    o_cg = o_ca + 2 * C_HEADS
    main = jnp.concatenate([w_in[..., :D_MODEL].astype(BF16), w_in[..., 2 * D_MODEL:o_bgk].astype(BF16),
                            w_in[..., o_bg:o_ca].astype(BF16), w_in[..., o_cg:].astype(BF16)], axis=-1)
    forget = w_in[..., D_MODEL:2 * D_MODEL]
    pad = jnp.zeros(w_in.shape[:-1] + (LANES - B_RANK - 2 * C_HEADS,), w_in.dtype)
    small = jnp.concatenate([w_in[..., o_bgk:o_bg], w_in[..., o_ca:o_cg], pad], axis=-1)
    return main, forget.astype(BF16), small.astype(F32)


def kernel(x, ln_mix_pre, ln_mix_post, ln_mlp_pre, ln_mlp_post, w_in, hgrn_lb_logits, gla_w_gk, gla_b_gk,
           gdn_conv, gdn_a_log, gdn_dt_bias, hgrn_onorm, gla_onorm, gdn_onorm, w_out, w_up, w_down):
    bsz, t_len, _ = x.shape
    n = bsz * t_len
    assert t_len % CHUNK == 0 and n % min(PROJ_TM, n) == 0 and n % min(MLP_TM, n) == 0

    lbp = _lb_params(hgrn_lb_logits)
    w_main, w_forget, w_small = _reorder_w_in(w_in)
    wgk = jnp.pad(gla_w_gk.astype(F32), ((0, 0), (0, LANES - B_RANK), (0, 0)))
    lane_pad = ((0, 0), (SM_A, LANES - SM_A - C_HEADS))
    head_params = jnp.stack([jnp.pad(gdn_a_log.astype(F32), lane_pad),
                             jnp.pad(gdn_dt_bias.astype(F32), lane_pad)], axis=1)
    head_params = jnp.pad(head_params, ((0, 0), (0, SUBLANES - 2), (0, 0)))
    ln_rest = jnp.stack([ln_mix_post, ln_mlp_pre, ln_mlp_post], axis=1).astype(F32)
    ln_rest = jnp.pad(ln_rest, ((0, 0), (0, SUBLANES - 3), (0, 0)))
    wo, wu, wd = w_out.astype(BF16), w_up.astype(BF16), w_down.astype(BF16)

    x2 = x.reshape(n, D_MODEL).astype(F32)
    for layer in range(DEPTH):
        proj, forget, small = _proj(x2, ln_mix_pre[layer][None, :].astype(F32), w_main[layer],
                                    w_forget[layer], w_small[layer])
        proj3 = proj.reshape(bsz, t_len, N_MAIN)
        small3 = small.reshape(bsz, t_len, LANES)
        ya = _mixer_a(proj3, forget.reshape(bsz, t_len, D_MODEL), lbp[layer],
                      hgrn_onorm[layer][None, :].astype(F32))
        yb = _mixer_b(proj3, small3, wgk[layer], gla_b_gk[layer][None, :].astype(F32),
                      gla_onorm[layer][None, :].astype(F32))
        yc = _mixer_c(proj3, small3, gdn_conv[layer].astype(F32), head_params[layer],
                      gdn_onorm[layer][None, :].astype(F32))
        x2 = _out_mlp(x2, ya.reshape(n, D_MODEL), yb.reshape(n, D_MODEL), yc.reshape(n, D_MODEL),
                      wo[layer], wu[layer], wd[layer], ln_rest[layer])
    return x2.reshape(bsz, t_len, D_MODEL).astype(x.dtype)
```

```python
import functools

import jax
import jax.numpy as jnp
from jax import lax
from jax.experimental import pallas as pl
from jax.experimental.pallas import tpu as pltpu

F32 = jnp.float32
BF16 = jnp.bfloat16

D_MODEL = 1024
DEPTH = 4
CHUNK = 64
CHUNK_LOG2 = 6
EPS = 1e-6
LOG2_E = 1.4426950408889634
LANES = 128
SUBLANES = 8

A_HEADS, A_DK, A_DV = 8, 128, 128
B_HEADS, B_DK, B_DV = 4, 128, 256
B_KEY = B_HEADS * B_DK
B_RANK = 16
B_GATE_NORM = 16.0
C_HEADS, C_DK, C_DV = 8, 128, 128
C_KEY = C_HEADS * C_DK
C_CONV = 4
C_QKV = 2 * C_KEY + D_MODEL
D_FF = 4 * D_MODEL

BLK_AQ, BLK_AI, BLK_AG = 0, 1, 2
BLK_BQK, BLK_BV, BLK_BG = 3, 4, 5
BLK_CQ, BLK_CK, BLK_CV, BLK_CG = 6, 7, 8, 9
BLK_M0, BLK_M1, BLK_M2 = 10, 11, 12
N_MAIN = 13 * D_MODEL
SM_GK, SM_A, SM_B = 0, B_RANK, B_RANK + C_HEADS

VMEM_LIMIT = 56 * 1024 * 1024


def _dot(a, b):
    return jnp.dot(a, b, preferred_element_type=F32)


def _dot_nt(a, b):
    return lax.dot_general(a, b, (((1,), (1,)), ((), ())), preferred_element_type=F32)


def _dot_tn(a, b):
    return lax.dot_general(a, b, (((0,), (0,)), ((), ())), preferred_element_type=F32)


def _split2(a):
    hi = a.astype(BF16)
    lo = (a - hi.astype(F32)).astype(BF16)
    return hi, lo


def _dot3(a, b):
    ah, al = _split2(a)
    bh, bl = _split2(b)
    return _dot(ah, bh) + _dot(ah, bl) + _dot(al, bh)


def _cumsum_rows(g):
    row = lax.broadcasted_iota(jnp.int32, (CHUNK, CHUNK), 0)
    col = lax.broadcasted_iota(jnp.int32, (CHUNK, CHUNK), 1)
    tril = jnp.where(row >= col, 1.0, 0.0).astype(BF16)
    g1 = g.astype(BF16)
    r1 = g - g1.astype(F32)
    g2 = r1.astype(BF16)
    g3 = (r1 - g2.astype(F32)).astype(BF16)
    return _dot(tril, g1) + _dot(tril, g2) + _dot(tril, g3)


def _sigmoid(x):
    return jax.nn.sigmoid(x)


def _silu(x):
    return x * jax.nn.sigmoid(x)


def _log1p_exp_neg_abs(x):
    return jnp.log(1.0 + jnp.exp(-jnp.abs(x)))


def _log_sigmoid(x):
    return jnp.minimum(x, 0.0) - _log1p_exp_neg_abs(x)


def _rms_rows(x, w):
    return x * lax.rsqrt(jnp.mean(x * x, axis=-1, keepdims=True) + EPS) * w


def _lb_kernel(logits_ref, out_ref):
    lg = logits_ref[...]
    e = jnp.exp(lg - jnp.max(lg, axis=0, keepdims=True))
    p = e / jnp.sum(e, axis=0, keepdims=True)
    cum = p[0:1]
    first = cum
    pad = jnp.zeros((SUBLANES - 3, lg.shape[1]), F32)
    for layer in range(DEPTH):
        if layer > 0:
            cum = cum + p[layer:layer + 1]
        lb = jnp.clip(cum - first, 0.0, 1.0)
        out_ref[layer] = jnp.concatenate([jnp.log(lb), jnp.log1p(-lb), 1.0 - lb, pad], axis=0)


def _lb_params(logits):
    return pl.pallas_call(
        _lb_kernel,
        out_shape=jax.ShapeDtypeStruct((DEPTH, SUBLANES, logits.shape[1]), F32),
        name="hgrn_lower_bounds",
    )(logits.astype(F32))


PROJ_TM = 1024
PROJ_TN = N_MAIN // 8


def _proj_kernel(x_ref, lnw_ref, w_ref, wf_ref, ws_ref, o_ref, of_ref, os_ref, h_ref):
    @pl.when(pl.program_id(1) == 0)
    def _():
        h = _rms_rows(x_ref[...], lnw_ref[...])
        hh, hl = _split2(h)
        h_ref[...] = hh
        wh, wl = _split2(ws_ref[...])
        os_ref[...] = _dot(hh, wh) + _dot(hh, wl) + _dot(hl, wh)
        of_ref[...] = _dot(hh, wf_ref[...])

    o_ref[...] = _dot(h_ref[...], w_ref[...]).astype(BF16)


def _proj(x2, lnw, w_main, w_forget, w_small):
    n = x2.shape[0]
    tm = min(PROJ_TM, n)

    def resident(shape):
        return pl.BlockSpec(shape, lambda i, j: (0, 0), pipeline_mode=pl.Buffered(1))

    return pl.pallas_call(
        _proj_kernel,
        grid=(n // tm, N_MAIN // PROJ_TN),
        in_specs=[
            pl.BlockSpec((tm, D_MODEL), lambda i, j: (i, 0)),
            resident((1, D_MODEL)),
            pl.BlockSpec((D_MODEL, PROJ_TN), lambda i, j: (0, j)),
            resident((D_MODEL, D_MODEL)),
            resident((D_MODEL, LANES)),
        ],
        out_specs=[
            pl.BlockSpec((tm, PROJ_TN), lambda i, j: (i, j)),
            pl.BlockSpec((tm, D_MODEL), lambda i, j: (i, 0)),
            pl.BlockSpec((tm, LANES), lambda i, j: (i, 0)),
        ],
        out_shape=[
            jax.ShapeDtypeStruct((n, N_MAIN), BF16),
            jax.ShapeDtypeStruct((n, D_MODEL), F32),
            jax.ShapeDtypeStruct((n, LANES), F32),
        ],
        scratch_shapes=[pltpu.VMEM((tm, D_MODEL), BF16)],
        compiler_params=pltpu.CompilerParams(
            dimension_semantics=("arbitrary", "arbitrary"), vmem_limit_bytes=VMEM_LIMIT),
        name="rmsnorm_in_proj",
    )(x2, lnw, w_main, w_forget, w_small)


def _chunk_masks():
    row = lax.broadcasted_iota(jnp.int32, (CHUNK, CHUNK), 0)
    col = lax.broadcasted_iota(jnp.int32, (CHUNK, CHUNK), 1)
    sibling = []
    for lvl in range(6):
        bi, bj = row >> lvl, col >> lvl
        sibling.append((bi == bj + 1) & ((bi & 1) == 1))
    return row == col, sibling


def _run_interleaved(gens, group, delay=None):
    results = [None] * len(gens)
    for lo in range(0, len(gens), group):
        live = list(enumerate(gens))[lo:lo + group]
        rnd = 0
        while live:
            still = []
            for i, gen in live:
                if delay is not None and rnd < delay[i]:
                    still.append((i, gen))
                    continue
                try:
                    next(gen)
                    still.append((i, gen))
                except StopIteration as stop:
                    results[i] = stop.value
            live = still
            rnd += 1
    return results


def _gla_head(q, k, v, g, st_ref, kb_ref, h, masks):
    eye, sibling = masks
    width = q.shape[1]
    g = g * LOG2_E
    b = _cumsum_rows(g)
    yield
    b_last = b[CHUNK - 1:CHUNK]
    qd = (q * jnp.exp2(b)).astype(BF16)
    kd = (k * jnp.exp2(b_last - b)).astype(BF16)
    s_decay = jnp.exp2(b_last)
    vb = v.astype(BF16)
    qb = q.astype(BF16)
    kb = k.astype(BF16)

    kb_ref[0] = b - g
    kb_ref[1] = b

    subl = lax.broadcasted_iota(jnp.int32, (SUBLANES, width), 0)

    def block_rows(idx, s, off):
        cache, pieces = {}, []

        def bcast(r):
            if r not in cache:
                cache[r] = jnp.broadcast_to(kb_ref[idx, pl.ds(r, 1), :], (SUBLANES, width))
            return cache[r]

        for base in range(0, CHUNK, SUBLANES):
            if s >= SUBLANES:
                pieces.append(bcast(s * (base // s) + off))
            else:
                n_blk = SUBLANES // s
                piece = bcast(base + (n_blk - 1) * s + off)
                for t in reversed(range(n_blk - 1)):
                    piece = jnp.where(subl < (t + 1) * s, bcast(base + t * s + off), piece)
                pieces.append(piece)
        return jnp.concatenate(pieces, axis=0)

    q_lv = [(q * jnp.exp2(g)).astype(BF16)]
    k_lv = [kb]
    odd = (subl & 1) == 1
    g_prev, g_next = [], []
    for base in range(0, CHUNK, SUBLANES):
        piece = g[base:base + SUBLANES]
        g_prev.append(jnp.where(odd, pltpu.roll(piece, 1, 0), 0.0))
        g_next.append(jnp.where(odd, 0.0, pltpu.roll(piece, SUBLANES - 1, 0)))
    q_lv.append((q * jnp.exp2(g + jnp.concatenate(g_prev, axis=0))).astype(BF16))
    k_lv.append((k * jnp.exp2(jnp.concatenate(g_next, axis=0))).astype(BF16))
    q_lv.append((q * jnp.exp2(b - block_rows(0, 4, 0))).astype(BF16))
    k_lv.append((k * jnp.exp2(block_rows(1, 4, 3) - b)).astype(BF16))
    def row_bcast(idx, r, n):
        return jnp.broadcast_to(kb_ref[idx, pl.ds(r, 1), :], (n, width))

    unit = 2 * SUBLANES
    later_rows = {}
    for s in (8, 16, 32):
        later_rows[s] = [lo for lo in range(0, CHUNK, SUBLANES) if (lo // s) & 1]
        q_rows = [q[lo:lo + SUBLANES] * jnp.exp2(b[lo:lo + SUBLANES] - row_bcast(0, lo // s * s, SUBLANES))
                  for lo in later_rows[s]]
        q_lv.append(jnp.concatenate(q_rows, axis=0).astype(BF16))
        k_units = []
        for lo in range(0, CHUNK, unit):
            if s == SUBLANES:
                early = slice(lo, lo + SUBLANES)
                scaled = k[early] * jnp.exp2(row_bcast(1, lo + SUBLANES - 1, SUBLANES) - b[early])
                k_units.append(jnp.concatenate([scaled, k[lo + SUBLANES:lo + unit]], axis=0).astype(BF16))
            elif (lo // s) & 1:
                k_units.append(kb[lo:lo + unit])
            else:
                rows = slice(lo, lo + unit)
                k_units.append((k[rows] * jnp.exp2(row_bcast(1, lo // s * s + s - 1, unit) - b[rows])).astype(BF16))
        k_lv.append(jnp.concatenate(k_units, axis=0))

    yield
    first = _dot_nt(jnp.concatenate([q_lv[0], qb], axis=0), kb)
    prods = [_dot_nt(q_lv[lvl], k_lv[lvl]) for lvl in range(1, 6)]
    st = st_ref[h]
    o_inter = _dot_nt(qd, st.astype(BF16))
    st_ref[h] = st * s_decay + _dot_tn(vb, kd)
    yield
    a = jnp.where(sibling[0], first[:CHUNK], jnp.where(eye, first[CHUNK:], 0.0))
    for lvl in (1, 2):
        a = jnp.where(sibling[lvl], prods[lvl - 1], a)
    a_rows = [a[lo:lo + SUBLANES] for lo in range(0, CHUNK, SUBLANES)]
    for lvl in (3, 4, 5):
        for i, lo in enumerate(later_rows[1 << lvl]):
            piece = prods[lvl - 1][i * SUBLANES:(i + 1) * SUBLANES]
            a_rows[lo // SUBLANES] = jnp.where(sibling[lvl][lo:lo + SUBLANES], piece, a_rows[lo // SUBLANES])
    ab = jnp.concatenate(a_rows, axis=0).astype(BF16)
    yield
    return o_inter + _dot(ab, vb)


def _head_out(outs, onorm, gate, merge):
    y = jnp.concatenate([_rms_rows(o, onorm) for o in outs], axis=-1)
    return y * _silu(gate) * _sigmoid(merge)


def _f32(ref, rows=slice(None), lanes=slice(None)):
    return ref[rows, lanes].astype(F32)


def _mixer_a_kernel(aq_ref, ai_ref, ag_ref, m_ref, af_ref, lbp_ref, onorm_ref, y_ref, st_ref, kb_ref):
    @pl.when(pl.program_id(1) == 0)
    def _():
        st_ref[...] = jnp.zeros_like(st_ref)

    masks = _chunk_masks()

    def head(c, h):
        rows = slice(c * CHUNK, (c + 1) * CHUNK)
        hs = slice(h * A_DK, (h + 1) * A_DK)
        z = af_ref[rows, hs]
        log_lb, log_1m_lb, one_m_lb = lbp_ref[0:1, hs], lbp_ref[1:2, hs], lbp_ref[2:3, hs]
        t = log_1m_lb + _log_sigmoid(z)
        g = jnp.maximum(log_lb, t) + _log1p_exp_neg_abs(log_lb - t)
        k = one_m_lb * _sigmoid(-z)
        q = _silu(_f32(aq_ref, rows, hs))
        o = yield from _gla_head(q, k, ai_ref[rows, hs], g, st_ref, kb_ref.at[c * A_HEADS + h], h, masks)
        yield
        y = _head_out([o], onorm_ref[...], _f32(ag_ref, rows, hs), _f32(m_ref, rows, hs))
        y_ref[rows, hs] = y.astype(BF16)

    _run_interleaved([head(c, h) for c in range(aq_ref.shape[0] // CHUNK) for h in range(A_HEADS)],
                     GLA_GROUP)


def _mixer_b_kernel(qk_ref, v_ref, bg_ref, m_ref, sm_ref, wgk_ref, bgk_ref, onorm_ref, y_ref, st_ref, kb_ref):
    @pl.when(pl.program_id(1) == 0)
    def _():
        st_ref[...] = jnp.zeros_like(st_ref)

    masks = _chunk_masks()

    def head(c, h):
        rows = slice(c * CHUNK, (c + 1) * CHUNK)
        hs = slice(h * B_DK, (h + 1) * B_DK)
        vs = slice(h * B_DV, (h + 1) * B_DV)
        q = _f32(qk_ref, rows, hs) * (B_DK ** -0.5)
        k = _f32(qk_ref, rows, slice(B_KEY + h * B_DK, B_KEY + (h + 1) * B_DK))
        smh, sml = _split2(sm_ref[rows, :])
        wh, wl = _split2(wgk_ref[:, hs])
        gk = _dot(smh, wh) + _dot(smh, wl) + _dot(sml, wh) + bgk_ref[:, hs]
        yield
        g = _log_sigmoid(gk) / B_GATE_NORM
        o = yield from _gla_head(q, k, v_ref[rows, vs], g, st_ref, kb_ref.at[c * B_HEADS + h], h, masks)
        yield
        y = _head_out([o], onorm_ref[...], _f32(bg_ref, rows, vs), _f32(m_ref, rows, vs))
        y_ref[rows, vs] = y.astype(BF16)

    _run_interleaved([head(c, h) for c in range(qk_ref.shape[0] // CHUNK) for h in range(B_HEADS)],
                     GLA_GROUP // 2)


def _tok_index(b, t, blk):
    return (b, t, blk)


def _full_spec(shape):
    return pl.BlockSpec(shape, lambda b, t: (0,) * len(shape))


def _mixer_call(kernel, name, proj3, tok_blocks, extra_inputs, extra_specs, scratch, tb=CHUNK):
    bsz, t_len, _ = proj3.shape
    assert t_len % tb == 0 and tb % CHUNK == 0
    return pl.pallas_call(
        kernel,
        grid=(bsz, t_len // tb),
        in_specs=[pl.BlockSpec((None, tb, D_MODEL), functools.partial(_tok_index, blk=blk))
                  for blk in tok_blocks] + extra_specs,
        out_specs=pl.BlockSpec((None, tb, D_MODEL), lambda b, t: (b, t, 0)),
        out_shape=jax.ShapeDtypeStruct((bsz, t_len, D_MODEL), BF16),
        scratch_shapes=scratch,
        compiler_params=pltpu.CompilerParams(
            dimension_semantics=("arbitrary", "arbitrary"), vmem_limit_bytes=VMEM_LIMIT),
        name=name,
    )(*([proj3] * len(tok_blocks)), *extra_inputs)


GLA_TB = 256
GLA_GROUP = 32


def _mixer_a(proj3, forget3, lbp, onorm):
    tb = min(GLA_TB, proj3.shape[1])
    return _mixer_call(
        _mixer_a_kernel, "hgrn2_mixer", proj3,
        [BLK_AQ, BLK_AI, BLK_AG, BLK_M0],
        [forget3, lbp, onorm],
        [pl.BlockSpec((None, tb, D_MODEL), lambda b, t: (b, t, 0)),
         _full_spec((SUBLANES, D_MODEL)), _full_spec((1, A_DV))],
        [pltpu.VMEM((A_HEADS, A_DV, A_DK), F32),
         pltpu.VMEM((tb // CHUNK * A_HEADS, 2, CHUNK, A_DK), F32)],
        tb=tb)


def _mixer_b(proj3, small3, wgk, bgk, onorm):
    tb = min(2 * GLA_TB, proj3.shape[1])
    return _mixer_call(
        _mixer_b_kernel, "gla_mixer", proj3,
        [BLK_BQK, BLK_BV, BLK_BG, BLK_M1],
        [small3, wgk, bgk, onorm],
        [pl.BlockSpec((None, tb, LANES), lambda b, t: (b, t, 0)),
         _full_spec((LANES, B_KEY)), _full_spec((1, B_KEY)), _full_spec((1, B_DV))],
        [pltpu.VMEM((B_HEADS, B_DV, B_DK), F32),
         pltpu.VMEM((tb // CHUNK * B_HEADS, 2, CHUNK, B_DK), F32)],
        tb=tb)


def _l2norm_rows(x):
    return x * lax.rsqrt(jnp.sum(x * x, axis=-1, keepdims=True) + EPS)


def _mixer_c_kernel(cq_ref, ck_ref, cv_ref, cg_ref, m_ref, sm_ref, conv_ref, hp_ref, onorm_ref,
                    y_ref, st_ref, xb_ref):
    hist = GDN_HIST

    @pl.when(pl.program_id(1) == 0)
    def _():
        st_ref[...] = jnp.zeros_like(st_ref)
        xb_ref[0:hist, :] = jnp.zeros((hist, C_QKV), xb_ref.dtype)

    tb = cq_ref.shape[0]
    xb_ref[hist:hist + tb, 0:C_KEY] = cq_ref[...]
    xb_ref[hist:hist + tb, C_KEY:2 * C_KEY] = ck_ref[...]
    xb_ref[hist:hist + tb, 2 * C_KEY:C_QKV] = cv_ref[...]

    row = lax.broadcasted_iota(jnp.int32, (CHUNK, 2 * CHUNK), 0)
    lane = lax.broadcasted_iota(jnp.int32, (CHUNK, 2 * CHUNK), 1)
    col = lane & (CHUNK - 1)
    even = lane < CHUNK
    eye = jnp.where(row == col, 1.0, 0.0).astype(F32)
    causal = row >= col
    merge_mask = []
    for lvl in range(6):
        bi, bj = row >> lvl, col >> lvl
        merge_mask.append((bi == bj + 1) & ((bi & 1) == 1))
    keep_even = jnp.where(even, 1.0, 0.0).astype(BF16)
    keep_odd = jnp.where(even, 0.0, 1.0).astype(BF16)

    def blockdiag(m):
        return jnp.concatenate([m * keep_even, m * keep_odd], axis=0)

    def blockdiag_wide(m0, m1):
        z0, z1 = jnp.zeros_like(m1), jnp.zeros_like(m0)
        return jnp.concatenate([jnp.concatenate([m0, z0], axis=1), jnp.concatenate([z1, m1], axis=1)], axis=0)

    n_delay = C_CONV - 1
    dr = lax.broadcasted_iota(jnp.int32, (n_delay * CHUNK, hist + CHUNK), 0)
    dc = lax.broadcasted_iota(jnp.int32, (n_delay * CHUNK, hist + CHUNK), 1)
    delay_sel = jnp.where(dc == hist + (dr & (CHUNK - 1)) - ((dr >> CHUNK_LOG2) + 1), 1.0, 0.0)
    delay_sel = delay_sel.astype(xb_ref.dtype)

    def conv_silu(c, lanes):
        window = xb_ref[c * CHUNK:c * CHUNK + hist + CHUNK, lanes]
        delayed = _dot(delay_sel, window)
        acc = window[hist:].astype(F32) * conv_ref[C_CONV - 1:C_CONV, lanes]
        for j in range(1, C_CONV):
            acc = acc + delayed[(j - 1) * CHUNK:j * CHUNK] * conv_ref[C_CONV - 1 - j:C_CONV - j, lanes]
        return _silu(acc)

    shared = []
    for c in range(tb // CHUNK):
        sm = sm_ref[c * CHUNK:(c + 1) * CHUNK, :]
        log2_a = (-LOG2_E) * jnp.exp(hp_ref[0:1, :]) * jax.nn.softplus(sm + hp_ref[1:2, :])
        b_all = _cumsum_rows(log2_a)
        shared.append(dict(
            beta=_sigmoid(sm), b=b_all, b_t=jnp.concatenate([b_all, b_all], axis=0).T, e=jnp.exp2(b_all),
            d=jnp.exp2(b_all[CHUNK - 1:CHUNK] - b_all)))

    def pair(c, p):
        sh = shared[c]
        heads = (2 * p, 2 * p + 1)
        q, k, rhs, qe, kdec, beta, b_col = [], [], [], [], [], [], []
        for h in heads:
            q_h = _l2norm_rows(conv_silu(c, slice(h * C_DK, (h + 1) * C_DK))) * (C_DK ** -0.5)
            k_h = _l2norm_rows(conv_silu(c, slice(C_KEY + h * C_DK, C_KEY + (h + 1) * C_DK)))
            v_h = conv_silu(c, slice(2 * C_KEY + h * C_DV, 2 * C_KEY + (h + 1) * C_DV))
            beta_h = sh["beta"][:, SM_B + h:SM_B + h + 1]
            e_b = sh["e"][:, SM_A + h:SM_A + h + 1]
            q.append(q_h.astype(BF16))
            k.append(k_h.astype(BF16))
            rhs.append(jnp.concatenate([v_h * beta_h, k_h * (beta_h * e_b)], axis=-1).astype(BF16))
            qe.append((q_h * e_b).astype(BF16))
            kdec.append((k_h * sh["d"][:, SM_A + h:SM_A + h + 1]).astype(BF16))
            beta.append(beta_h)
            b_col.append(sh["b"][:, SM_A + h:SM_A + h + 1])
        yield
        b_row = jnp.where(even[0:1], sh["b_t"][SM_A + heads[0]:SM_A + heads[0] + 1, :],
                          sh["b_t"][SM_A + heads[1]:SM_A + heads[1] + 1, :])
        decay = jnp.where(causal, jnp.exp2(jnp.minimum(jnp.where(even, b_col[0], b_col[1]) - b_row, 0.0)), 0.0)
        k_diag = blockdiag_wide(k[0], k[1])
        kk = _dot_nt(jnp.concatenate(k, axis=1), k_diag)
        qk = _dot_nt(jnp.concatenate(q, axis=1), k_diag)
        yield
        lower = jnp.where(even, beta[0], beta[1]) * kk * decay
        qkd = (qk * decay).astype(BF16)
        x = eye - jnp.where(merge_mask[0], lower, 0.0)
        for lvl in range(1, 6):
            cpart = jnp.where(merge_mask[lvl], lower, 0.0).astype(BF16)
            xb = x.astype(BF16)
            y = _dot(cpart, blockdiag(xb)).astype(BF16)
            yield
            x = x - _dot(xb, blockdiag(y))
            yield
        sol = _dot(x.astype(BF16), blockdiag_wide(rhs[0], rhs[1]))
        width = C_DV + C_DK
        u = [sol[:, i * width:i * width + C_DV] for i in range(2)]
        w = [sol[:, i * width + C_DV:(i + 1) * width].astype(BF16) for i in range(2)]
        return u, w, qe, qkd, kdec

    def recur(c, p, u, w, qe, qkd, kdec):
        rows = slice(c * CHUNK, (c + 1) * CHUNK)
        heads = (2 * p, 2 * p + 1)
        lanes = slice(heads[0] * C_DV, (heads[1] + 1) * C_DV)
        st = [st_ref[h] for h in heads]
        stb = [s.astype(BF16) for s in st]
        yield
        v_new = [(u[i] - _dot_nt(w[i], stb[i])).astype(BF16) for i in range(2)]
        yield
        o_intra = _dot(qkd, blockdiag_wide(v_new[0], v_new[1]))
        o = [_dot_nt(qe[i], stb[i]) + o_intra[:, i * C_DV:(i + 1) * C_DV] for i in range(2)]
        for i, h in enumerate(heads):
            a_last = shared[c]["e"][CHUNK - 1:CHUNK, SM_A + h:SM_A + h + 1]
            st_ref[h] = a_last * st[i] + _dot_tn(v_new[i], kdec[i])
        yield
        y_ref[rows, lanes] = _head_out(o, onorm_ref[...], _f32(cg_ref, rows, lanes),
                                       _f32(m_ref, rows, lanes)).astype(BF16)

    n_chunk, n_pair = tb // CHUNK, C_HEADS // 2
    pre = _run_interleaved([pair(c, p) for c in range(n_chunk) for p in range(n_pair)], GDN_GROUP)
    xb_ref[0:hist, :] = xb_ref[tb:tb + hist, :]
    for c in range(n_chunk):
        _run_interleaved([recur(c, p, *pre[c * n_pair + p]) for p in range(n_pair)], n_pair)


GDN_TB = 512
GDN_HIST = CHUNK
GDN_GROUP = 32


def _mixer_c(proj3, small3, conv, head_params, onorm):
    tb = min(GDN_TB, proj3.shape[1])
    return _mixer_call(
        _mixer_c_kernel, "gated_deltanet_mixer", proj3,
        [BLK_CQ, BLK_CK, BLK_CV, BLK_CG, BLK_M2],
        [small3, conv, head_params, onorm],
        [pl.BlockSpec((None, tb, LANES), lambda b, t: (b, t, 0)),
         _full_spec((C_CONV, C_QKV)), _full_spec((SUBLANES, LANES)), _full_spec((1, C_DV))],
        [pltpu.VMEM((C_HEADS, C_DV, C_DK), F32), pltpu.VMEM((GDN_HIST + tb, C_QKV), proj3.dtype)],
        tb=tb)


MLP_TM = 512
MLP_FF_CHUNK = 1024


def _out_mlp_kernel(x_ref, ya_ref, yb_ref, yc_ref, wo_ref, wu_ref, wd_ref, ln_ref, o_ref):
    y = (_f32(ya_ref) + _f32(yb_ref) + _f32(yc_ref)).astype(BF16)
    mix = _dot(y, wo_ref[...])
    x1 = x_ref[...] + _rms_rows(mix, ln_ref[0:1, :])
    h = _rms_rows(x1, ln_ref[1:2, :]).astype(BF16)
    down = None
    for lo in range(0, D_FF, MLP_FF_CHUNK):
        up = _dot(h, wu_ref[:, lo:lo + MLP_FF_CHUNK])
        act = jnp.square(jnp.maximum(up, 0.0)).astype(BF16)
        part = _dot(act, wd_ref[lo:lo + MLP_FF_CHUNK, :])
        down = part if down is None else down + part
    o_ref[...] = x1 + _rms_rows(down, ln_ref[2:3, :])


def _out_mlp(x2, ya, yb, yc, wo, wu, wd, ln):
    n = x2.shape[0]
    tm = min(MLP_TM, n)
    tok = pl.BlockSpec((tm, D_MODEL), lambda i: (i, 0))

    def resident(shape):
        return pl.BlockSpec(shape, lambda i: (0, 0), pipeline_mode=pl.Buffered(1))

    return pl.pallas_call(
        _out_mlp_kernel,
        grid=(n // tm,),
        in_specs=[tok, tok, tok, tok,
                  resident((D_MODEL, D_MODEL)), resident((D_MODEL, D_FF)), resident((D_FF, D_MODEL)),
                  resident((SUBLANES, D_MODEL))],
        out_specs=tok,
        out_shape=jax.ShapeDtypeStruct((n, D_MODEL), F32),
        compiler_params=pltpu.CompilerParams(
            dimension_semantics=("arbitrary",), vmem_limit_bytes=VMEM_LIMIT),
        name="merge_out_proj_mlp",
    )(x2, ya, yb, yc, wo, wu, wd, ln)


def _reorder_w_in(w_in):
    o_bgk = 4 * D_MODEL + 2 * B_KEY + D_MODEL
    o_bg = o_bgk + B_RANK
    o_cqkv = o_bg + D_MODEL
    o_ca = o_cqkv + C_QKV
    o_cg = o_ca + 2 * C_HEADS
    main = jnp.concatenate([w_in[..., :D_MODEL].astype(BF16), w_in[..., 2 * D_MODEL:o_bgk].astype(BF16),
                            w_in[..., o_bg:o_ca].astype(BF16), w_in[..., o_cg:].astype(BF16)], axis=-1)
    forget = w_in[..., D_MODEL:2 * D_MODEL]
    pad = jnp.zeros(w_in.shape[:-1] + (LANES - B_RANK - 2 * C_HEADS,), w_in.dtype)
    small = jnp.concatenate([w_in[..., o_bgk:o_bg], w_in[..., o_ca:o_cg], pad], axis=-1)
    return main, forget.astype(BF16), small.astype(F32)


def kernel(x, ln_mix_pre, ln_mix_post, ln_mlp_pre, ln_mlp_post, w_in, hgrn_lb_logits, gla_w_gk, gla_b_gk,
           gdn_conv, gdn_a_log, gdn_dt_bias, hgrn_onorm, gla_onorm, gdn_onorm, w_out, w_up, w_down):
    bsz, t_len, _ = x.shape
    n = bsz * t_len
    assert t_len % CHUNK == 0 and n % min(PROJ_TM, n) == 0 and n % min(MLP_TM, n) == 0

    lbp = _lb_params(hgrn_lb_logits)
    w_main, w_forget, w_small = _reorder_w_in(w_in)
    wgk = jnp.pad(gla_w_gk.astype(F32), ((0, 0), (0, LANES - B_RANK), (0, 0)))
    lane_pad = ((0, 0), (SM_A, LANES - SM_A - C_HEADS))
    head_params = jnp.stack([jnp.pad(gdn_a_log.astype(F32), lane_pad),
                             jnp.pad(gdn_dt_bias.astype(F32), lane_pad)], axis=1)
    head_params = jnp.pad(head_params, ((0, 0), (0, SUBLANES - 2), (0, 0)))
    ln_rest = jnp.stack([ln_mix_post, ln_mlp_pre, ln_mlp_post], axis=1).astype(F32)
    ln_rest = jnp.pad(ln_rest, ((0, 0), (0, SUBLANES - 3), (0, 0)))
    wo, wu, wd = w_out.astype(BF16), w_up.astype(BF16), w_down.astype(BF16)

    x2 = x.reshape(n, D_MODEL).astype(F32)
    for layer in range(DEPTH):
        proj, forget, small = _proj(x2, ln_mix_pre[layer][None, :].astype(F32), w_main[layer],
                                    w_forget[layer], w_small[layer])
        proj3 = proj.reshape(bsz, t_len, N_MAIN)
        small3 = small.reshape(bsz, t_len, LANES)
        ya = _mixer_a(proj3, forget.reshape(bsz, t_len, D_MODEL), lbp[layer],
                      hgrn_onorm[layer][None, :].astype(F32))
        yb = _mixer_b(proj3, small3, wgk[layer], gla_b_gk[layer][None, :].astype(F32),
                      gla_onorm[layer][None, :].astype(F32))
        yc = _mixer_c(proj3, small3, gdn_conv[layer].astype(F32), head_params[layer],
                      gdn_onorm[layer][None, :].astype(F32))
        x2 = _out_mlp(x2, ya.reshape(n, D_MODEL), yb.reshape(n, D_MODEL), yc.reshape(n, D_MODEL),
                      wo[layer], wu[layer], wd[layer], ln_rest[layer])
    return x2.reshape(bsz, t_len, D_MODEL).astype(x.dtype)
```

```python
import functools

import jax
import jax.numpy as jnp
from jax import lax
from jax.experimental import pallas as pl
from jax.experimental.pallas import tpu as pltpu

F32 = jnp.float32
BF16 = jnp.bfloat16

D_MODEL = 1024
DEPTH = 4
CHUNK = 64
CHUNK_LOG2 = CHUNK.bit_length() - 1
EPS = 1e-6
LOG2_E = 1.4426950408889634
LANES = 128
SUBLANES = 8

A_HEADS, A_DK, A_DV = 8, 128, 128
B_HEADS, B_DK, B_DV = 4, 128, 256
B_KEY = B_HEADS * B_DK
B_RANK = 16
B_GATE_NORM = 16.0
C_HEADS, C_DK, C_DV = 8, 128, 128
C_KEY = C_HEADS * C_DK
C_CONV = 4
C_QKV = 2 * C_KEY + D_MODEL
D_FF = 4 * D_MODEL

BLK_AQ, BLK_AI, BLK_AG = 0, 1, 2
BLK_BQK, BLK_BV, BLK_BG = 3, 4, 5
BLK_CQ, BLK_CK, BLK_CV, BLK_CG = 6, 7, 8, 9
BLK_M0, BLK_M1, BLK_M2 = 10, 11, 12
N_MAIN = 13 * D_MODEL
SM_GK, SM_A, SM_B = 0, B_RANK, B_RANK + C_HEADS

VMEM_LIMIT = 56 * 1024 * 1024


def _dot(a, b):
    return jnp.dot(a, b, preferred_element_type=F32)


def _dot_nt(a, b):
    return lax.dot_general(a, b, (((1,), (1,)), ((), ())), preferred_element_type=F32)


def _dot_tn(a, b):
    return lax.dot_general(a, b, (((0,), (0,)), ((), ())), preferred_element_type=F32)


def _split2(a):
    hi = a.astype(BF16)
    lo = (a - hi.astype(F32)).astype(BF16)
    return hi, lo


def _cumsum_rows(g):
    row = lax.broadcasted_iota(jnp.int32, (CHUNK, CHUNK), 0)
    col = lax.broadcasted_iota(jnp.int32, (CHUNK, CHUNK), 1)
    tril = jnp.where(row >= col, 1.0, 0.0).astype(BF16)
    g1 = g.astype(BF16)
    r1 = g - g1.astype(F32)
    g2 = r1.astype(BF16)
    g3 = (r1 - g2.astype(F32)).astype(BF16)
    return _dot(tril, g1) + _dot(tril, g2) + _dot(tril, g3)


def _sigmoid(x):
    return jax.nn.sigmoid(x)


def _silu(x):
    return x * jax.nn.sigmoid(x)


def _log1p_exp_neg_abs(x):
    return jnp.log(1.0 + jnp.exp(-jnp.abs(x)))


def _log_sigmoid(x):
    return jnp.minimum(x, 0.0) - _log1p_exp_neg_abs(x)


def _rms_rows(x, w):
    return x * lax.rsqrt(jnp.mean(x * x, axis=-1, keepdims=True) + EPS) * w


def _lb_kernel(logits_ref, out_ref):
    lg = logits_ref[...]
    e = jnp.exp(lg - jnp.max(lg, axis=0, keepdims=True))
    p = e / jnp.sum(e, axis=0, keepdims=True)
    cum = p[0:1]
    first = cum
    pad = jnp.zeros((SUBLANES - 3, lg.shape[1]), F32)
    for layer in range(DEPTH):
        if layer > 0:
            cum = cum + p[layer:layer + 1]
        lb = jnp.clip(cum - first, 0.0, 1.0)
        out_ref[layer] = jnp.concatenate([jnp.log(lb), jnp.log1p(-lb), 1.0 - lb, pad], axis=0)


def _lb_params(logits):
    return pl.pallas_call(
        _lb_kernel,
        out_shape=jax.ShapeDtypeStruct((DEPTH, SUBLANES, logits.shape[1]), F32),
        name="hgrn_lower_bounds",
    )(logits.astype(F32))


PROJ_TM = 1024
PROJ_TN = N_MAIN // 8


def _proj_kernel(x_ref, lnw_ref, w_ref, wf_ref, ws_ref, o_ref, of_ref, os_ref, h_ref):
    @pl.when(pl.program_id(1) == 0)
    def _():
        h = _rms_rows(x_ref[...], lnw_ref[...])
        hh, hl = _split2(h)
        h_ref[...] = hh
        wh, wl = _split2(ws_ref[...])
        os_ref[...] = _dot(hh, wh) + _dot(hh, wl) + _dot(hl, wh)
        of_ref[...] = _dot(hh, wf_ref[...])

    o_ref[...] = _dot(h_ref[...], w_ref[...]).astype(BF16)


def _proj(x2, lnw, w_main, w_forget, w_small):
    n = x2.shape[0]
    tm = min(PROJ_TM, n)

    def resident(shape):
        return pl.BlockSpec(shape, lambda i, j: (0, 0), pipeline_mode=pl.Buffered(1))

    return pl.pallas_call(
        _proj_kernel,
        grid=(n // tm, N_MAIN // PROJ_TN),
        in_specs=[
            pl.BlockSpec((tm, D_MODEL), lambda i, j: (i, 0)),
            resident((1, D_MODEL)),
            pl.BlockSpec((D_MODEL, PROJ_TN), lambda i, j: (0, j)),
            resident((D_MODEL, D_MODEL)),
            resident((D_MODEL, LANES)),
        ],
        out_specs=[
            pl.BlockSpec((tm, PROJ_TN), lambda i, j: (i, j)),
            pl.BlockSpec((tm, D_MODEL), lambda i, j: (i, 0)),
            pl.BlockSpec((tm, LANES), lambda i, j: (i, 0)),
        ],
        out_shape=[
            jax.ShapeDtypeStruct((n, N_MAIN), BF16),
            jax.ShapeDtypeStruct((n, D_MODEL), F32),
            jax.ShapeDtypeStruct((n, LANES), F32),
        ],
        scratch_shapes=[pltpu.VMEM((tm, D_MODEL), BF16)],
        compiler_params=pltpu.CompilerParams(
            dimension_semantics=("arbitrary", "arbitrary"), vmem_limit_bytes=VMEM_LIMIT),
        name="rmsnorm_in_proj",
    )(x2, lnw, w_main, w_forget, w_small)


def _chunk_masks():
    row = lax.broadcasted_iota(jnp.int32, (CHUNK, CHUNK), 0)
    col = lax.broadcasted_iota(jnp.int32, (CHUNK, CHUNK), 1)
    sibling = []
    for lvl in range(6):
        bi, bj = row >> lvl, col >> lvl
        sibling.append((bi == bj + 1) & ((bi & 1) == 1))
    return row == col, sibling


def _run_interleaved(gens, group):
    results = [None] * len(gens)
    for lo in range(0, len(gens), group):
        live = list(enumerate(gens))[lo:lo + group]
        while live:
            still = []
            for i, gen in live:
                try:
                    next(gen)
                    still.append((i, gen))
                except StopIteration as stop:
                    results[i] = stop.value
            live = still
    return results


def _gla_head(q, k, v, g, st_ref, kb_ref, h, masks):
    eye, sibling = masks
    width = q.shape[1]
    g = g * LOG2_E
    b = _cumsum_rows(g)
    yield
    b_last = b[CHUNK - 1:CHUNK]
    qd = (q * jnp.exp2(b)).astype(BF16)
    kd = (k * jnp.exp2(b_last - b)).astype(BF16)
    s_decay = jnp.exp2(b_last)
    vb = v.astype(BF16)
    qb = q.astype(BF16)
    kb = k.astype(BF16)

    kb_ref[0] = b - g
    kb_ref[1] = b

    subl = lax.broadcasted_iota(jnp.int32, (SUBLANES, width), 0)

    def row_bcast(idx, r, n):
        return jnp.broadcast_to(kb_ref[idx, pl.ds(r, 1), :], (n, width))

    half = SUBLANES // 2

    def half_group_rows(idx, off):
        return jnp.concatenate(
            [jnp.where(subl < half, row_bcast(idx, base + off, SUBLANES), row_bcast(idx, base + half + off, SUBLANES))
             for base in range(0, CHUNK, SUBLANES)], axis=0)

    q_lv = [(q * jnp.exp2(g)).astype(BF16)]
    k_lv = [kb]
    odd = (subl & 1) == 1
    g_prev, g_next = [], []
    for base in range(0, CHUNK, SUBLANES):
        piece = g[base:base + SUBLANES]
        g_prev.append(jnp.where(odd, pltpu.roll(piece, 1, 0), 0.0))
        g_next.append(jnp.where(odd, 0.0, pltpu.roll(piece, SUBLANES - 1, 0)))
    q_lv.append((q * jnp.exp2(g + jnp.concatenate(g_prev, axis=0))).astype(BF16))
    k_lv.append((k * jnp.exp2(jnp.concatenate(g_next, axis=0))).astype(BF16))
    q_lv.append((q * jnp.exp2(b - half_group_rows(0, 0))).astype(BF16))
    k_lv.append((k * jnp.exp2(half_group_rows(1, half - 1) - b)).astype(BF16))
    unit = 2 * SUBLANES
    later_rows = {}
    for s in (8, 16, 32):
        later_rows[s] = [lo for lo in range(0, CHUNK, SUBLANES) if (lo // s) & 1]
        q_rows = [q[lo:lo + SUBLANES] * jnp.exp2(b[lo:lo + SUBLANES] - row_bcast(0, lo // s * s, SUBLANES))
                  for lo in later_rows[s]]
        q_lv.append(jnp.concatenate(q_rows, axis=0).astype(BF16))
        k_units = []
        for lo in range(0, CHUNK, unit):
            if s == SUBLANES:
                early = slice(lo, lo + SUBLANES)
                scaled = k[early] * jnp.exp2(row_bcast(1, lo + SUBLANES - 1, SUBLANES) - b[early])
                k_units.append(jnp.concatenate([scaled, k[lo + SUBLANES:lo + unit]], axis=0).astype(BF16))
            elif (lo // s) & 1:
                k_units.append(kb[lo:lo + unit])
            else:
                rows = slice(lo, lo + unit)
                k_units.append((k[rows] * jnp.exp2(row_bcast(1, lo // s * s + s - 1, unit) - b[rows])).astype(BF16))
        k_lv.append(jnp.concatenate(k_units, axis=0))

    yield
    first = _dot_nt(jnp.concatenate([q_lv[0], qb], axis=0), kb)
    prods = [_dot_nt(q_lv[lvl], k_lv[lvl]) for lvl in range(1, 6)]
    st = st_ref[h]
    o_inter = _dot_nt(qd, st.astype(BF16))
    st_ref[h] = st * s_decay + _dot_tn(vb, kd)
    yield
    a = jnp.where(sibling[0], first[:CHUNK], jnp.where(eye, first[CHUNK:], 0.0))
    for lvl in (1, 2):
        a = jnp.where(sibling[lvl], prods[lvl - 1], a)
    a_rows = [a[lo:lo + SUBLANES] for lo in range(0, CHUNK, SUBLANES)]
    for lvl in (3, 4, 5):
        for i, lo in enumerate(later_rows[1 << lvl]):
            piece = prods[lvl - 1][i * SUBLANES:(i + 1) * SUBLANES]
            a_rows[lo // SUBLANES] = jnp.where(sibling[lvl][lo:lo + SUBLANES], piece, a_rows[lo // SUBLANES])
    ab = jnp.concatenate(a_rows, axis=0).astype(BF16)
    yield
    return o_inter + _dot(ab, vb)


def _head_out(outs, onorm, gate, merge):
    y = jnp.concatenate([_rms_rows(o, onorm) for o in outs], axis=-1)
    return y * _silu(gate) * _sigmoid(merge)


def _f32(ref, rows=slice(None), lanes=slice(None)):
    return ref[rows, lanes].astype(F32)


def _mixer_a_kernel(aq_ref, ai_ref, ag_ref, m_ref, af_ref, lbp_ref, onorm_ref, y_ref, st_ref, kb_ref):
    @pl.when(pl.program_id(1) == 0)
    def _():
        st_ref[...] = jnp.zeros_like(st_ref)

    masks = _chunk_masks()

    def head(c, h):
        rows = slice(c * CHUNK, (c + 1) * CHUNK)
        hs = slice(h * A_DK, (h + 1) * A_DK)
        z = af_ref[rows, hs]
        log_lb, log_1m_lb, one_m_lb = lbp_ref[0:1, hs], lbp_ref[1:2, hs], lbp_ref[2:3, hs]
        t = log_1m_lb + _log_sigmoid(z)
        g = jnp.maximum(log_lb, t) + _log1p_exp_neg_abs(log_lb - t)
        k = one_m_lb * _sigmoid(-z)
        q = _silu(_f32(aq_ref, rows, hs))
        o = yield from _gla_head(q, k, ai_ref[rows, hs], g, st_ref, kb_ref.at[c * A_HEADS + h], h, masks)
        yield
        y = _head_out([o], onorm_ref[...], _f32(ag_ref, rows, hs), _f32(m_ref, rows, hs))
        y_ref[rows, hs] = y.astype(BF16)

    _run_interleaved([head(c, h) for c in range(aq_ref.shape[0] // CHUNK) for h in range(A_HEADS)],
                     GLA_GROUP)


def _mixer_b_kernel(qk_ref, v_ref, bg_ref, m_ref, sm_ref, wgk_ref, bgk_ref, onorm_ref, y_ref, st_ref, kb_ref):
    @pl.when(pl.program_id(1) == 0)
    def _():
        st_ref[...] = jnp.zeros_like(st_ref)

    masks = _chunk_masks()

    def head(c, h):
        rows = slice(c * CHUNK, (c + 1) * CHUNK)
        hs = slice(h * B_DK, (h + 1) * B_DK)
        vs = slice(h * B_DV, (h + 1) * B_DV)
        q = _f32(qk_ref, rows, hs) * (B_DK ** -0.5)
        k = _f32(qk_ref, rows, slice(B_KEY + h * B_DK, B_KEY + (h + 1) * B_DK))
        smh, sml = _split2(sm_ref[rows, :])
        wh, wl = _split2(wgk_ref[:, hs])
        gk = _dot(smh, wh) + _dot(smh, wl) + _dot(sml, wh) + bgk_ref[:, hs]
        yield
        g = _log_sigmoid(gk) / B_GATE_NORM
        o = yield from _gla_head(q, k, v_ref[rows, vs], g, st_ref, kb_ref.at[c * B_HEADS + h], h, masks)
        yield
        y = _head_out([o], onorm_ref[...], _f32(bg_ref, rows, vs), _f32(m_ref, rows, vs))
        y_ref[rows, vs] = y.astype(BF16)

    _run_interleaved([head(c, h) for c in range(qk_ref.shape[0] // CHUNK) for h in range(B_HEADS)],
                     GLA_GROUP // 2)


def _tok_index(b, t, blk):
    return (b, t, blk)


def _full_spec(shape):
    return pl.BlockSpec(shape, lambda b, t: (0,) * len(shape))


def _mixer_call(kernel, name, proj3, tok_blocks, extra_inputs, extra_specs, scratch, tb=CHUNK):
    bsz, t_len, _ = proj3.shape
    assert t_len % tb == 0 and tb % CHUNK == 0
    return pl.pallas_call(
        kernel,
        grid=(bsz, t_len // tb),
        in_specs=[pl.BlockSpec((None, tb, D_MODEL), functools.partial(_tok_index, blk=blk))
                  for blk in tok_blocks] + extra_specs,
        out_specs=pl.BlockSpec((None, tb, D_MODEL), lambda b, t: (b, t, 0)),
        out_shape=jax.ShapeDtypeStruct((bsz, t_len, D_MODEL), BF16),
        scratch_shapes=scratch,
        compiler_params=pltpu.CompilerParams(
            dimension_semantics=("arbitrary", "arbitrary"), vmem_limit_bytes=VMEM_LIMIT),
        name=name,
    )(*([proj3] * len(tok_blocks)), *extra_inputs)


GLA_TB = 256
GLA_GROUP = 32


def _mixer_a(proj3, forget3, lbp, onorm):
    tb = min(GLA_TB, proj3.shape[1])
    return _mixer_call(
        _mixer_a_kernel, "hgrn2_mixer", proj3,
        [BLK_AQ, BLK_AI, BLK_AG, BLK_M0],
        [forget3, lbp, onorm],
        [pl.BlockSpec((None, tb, D_MODEL), lambda b, t: (b, t, 0)),
         _full_spec((SUBLANES, D_MODEL)), _full_spec((1, A_DV))],
        [pltpu.VMEM((A_HEADS, A_DV, A_DK), F32),
         pltpu.VMEM((tb // CHUNK * A_HEADS, 2, CHUNK, A_DK), F32)],
        tb=tb)


def _mixer_b(proj3, small3, wgk, bgk, onorm):
    tb = min(2 * GLA_TB, proj3.shape[1])
    return _mixer_call(
        _mixer_b_kernel, "gla_mixer", proj3,
        [BLK_BQK, BLK_BV, BLK_BG, BLK_M1],
        [small3, wgk, bgk, onorm],
        [pl.BlockSpec((None, tb, LANES), lambda b, t: (b, t, 0)),
         _full_spec((LANES, B_KEY)), _full_spec((1, B_KEY)), _full_spec((1, B_DV))],
        [pltpu.VMEM((B_HEADS, B_DV, B_DK), F32),
         pltpu.VMEM((tb // CHUNK * B_HEADS, 2, CHUNK, B_DK), F32)],
        tb=tb)


def _l2norm_rows(x):
    return x * lax.rsqrt(jnp.sum(x * x, axis=-1, keepdims=True) + EPS)


def _mixer_c_kernel(cq_ref, ck_ref, cv_ref, cg_ref, m_ref, sm_ref, conv_ref, hp_ref, onorm_ref,
                    y_ref, st_ref, xb_ref):
    hist = GDN_HIST

    @pl.when(pl.program_id(1) == 0)
    def _():
        st_ref[...] = jnp.zeros_like(st_ref)
        xb_ref[0:hist, :] = jnp.zeros((hist, C_QKV), xb_ref.dtype)

    tb = cq_ref.shape[0]
    xb_ref[hist:hist + tb, 0:C_KEY] = cq_ref[...]
    xb_ref[hist:hist + tb, C_KEY:2 * C_KEY] = ck_ref[...]
    xb_ref[hist:hist + tb, 2 * C_KEY:C_QKV] = cv_ref[...]

    row = lax.broadcasted_iota(jnp.int32, (CHUNK, 2 * CHUNK), 0)
    lane = lax.broadcasted_iota(jnp.int32, (CHUNK, 2 * CHUNK), 1)
    col = lane & (CHUNK - 1)
    even = lane < CHUNK
    eye = jnp.where(row == col, 1.0, 0.0).astype(F32)
    causal = row >= col
    merge_mask = []
    for lvl in range(6):
        bi, bj = row >> lvl, col >> lvl
        merge_mask.append((bi == bj + 1) & ((bi & 1) == 1))
    keep_even = jnp.where(even, 1.0, 0.0).astype(BF16)
    keep_odd = jnp.where(even, 0.0, 1.0).astype(BF16)

    def blockdiag(m):
        return jnp.concatenate([m * keep_even, m * keep_odd], axis=0)

    def blockdiag_wide(m0, m1):
        z0, z1 = jnp.zeros_like(m1), jnp.zeros_like(m0)
        return jnp.concatenate([jnp.concatenate([m0, z0], axis=1), jnp.concatenate([z1, m1], axis=1)], axis=0)

    n_delay = C_CONV - 1
    dr = lax.broadcasted_iota(jnp.int32, (n_delay * CHUNK, hist + CHUNK), 0)
    dc = lax.broadcasted_iota(jnp.int32, (n_delay * CHUNK, hist + CHUNK), 1)
    delay_sel = jnp.where(dc == hist + (dr & (CHUNK - 1)) - ((dr >> CHUNK_LOG2) + 1), 1.0, 0.0)
    delay_sel = delay_sel.astype(xb_ref.dtype)

    def conv_silu(c, lanes):
        window = xb_ref[c * CHUNK:c * CHUNK + hist + CHUNK, lanes]
        delayed = _dot(delay_sel, window)
        acc = window[hist:].astype(F32) * conv_ref[C_CONV - 1:C_CONV, lanes]
        for j in range(1, C_CONV):
            acc = acc + delayed[(j - 1) * CHUNK:j * CHUNK] * conv_ref[C_CONV - 1 - j:C_CONV - j, lanes]
        return _silu(acc)

    shared = []
    for c in range(tb // CHUNK):
        sm = sm_ref[c * CHUNK:(c + 1) * CHUNK, :]
        log2_a = (-LOG2_E) * jnp.exp(hp_ref[0:1, :]) * jax.nn.softplus(sm + hp_ref[1:2, :])
        b_all = _cumsum_rows(log2_a)
        shared.append(dict(
            beta=_sigmoid(sm), b=b_all, b_t=jnp.concatenate([b_all, b_all], axis=0).T, e=jnp.exp2(b_all),
            d=jnp.exp2(b_all[CHUNK - 1:CHUNK] - b_all)))

    def pair(c, p):
        sh = shared[c]
        heads = (2 * p, 2 * p + 1)
        q, k, rhs, qe, kdec, beta, b_col = [], [], [], [], [], [], []
        for h in heads:
            q_h = _l2norm_rows(conv_silu(c, slice(h * C_DK, (h + 1) * C_DK))) * (C_DK ** -0.5)
            k_h = _l2norm_rows(conv_silu(c, slice(C_KEY + h * C_DK, C_KEY + (h + 1) * C_DK)))
            v_h = conv_silu(c, slice(2 * C_KEY + h * C_DV, 2 * C_KEY + (h + 1) * C_DV))
            beta_h = sh["beta"][:, SM_B + h:SM_B + h + 1]
            e_b = sh["e"][:, SM_A + h:SM_A + h + 1]
            q.append(q_h.astype(BF16))
            k.append(k_h.astype(BF16))
            rhs.append(jnp.concatenate([v_h * beta_h, k_h * (beta_h * e_b)], axis=-1).astype(BF16))
            qe.append((q_h * e_b).astype(BF16))
            kdec.append((k_h * sh["d"][:, SM_A + h:SM_A + h + 1]).astype(BF16))
            beta.append(beta_h)
            b_col.append(sh["b"][:, SM_A + h:SM_A + h + 1])
        yield
        b_row = jnp.where(even[0:1], sh["b_t"][SM_A + heads[0]:SM_A + heads[0] + 1, :],
                          sh["b_t"][SM_A + heads[1]:SM_A + heads[1] + 1, :])
        decay = jnp.where(causal, jnp.exp2(jnp.minimum(jnp.where(even, b_col[0], b_col[1]) - b_row, 0.0)), 0.0)
        k_diag = blockdiag_wide(k[0], k[1])
        kk = _dot_nt(jnp.concatenate(k, axis=1), k_diag)
        qk = _dot_nt(jnp.concatenate(q, axis=1), k_diag)
        yield
        lower = jnp.where(even, beta[0], beta[1]) * kk * decay
        qkd = (qk * decay).astype(BF16)
        x = eye - jnp.where(merge_mask[0], lower, 0.0)
        for lvl in range(1, 6):
            cpart = jnp.where(merge_mask[lvl], lower, 0.0).astype(BF16)
            xb = x.astype(BF16)
            y = _dot(cpart, blockdiag(xb)).astype(BF16)
            yield
            x = x - _dot(xb, blockdiag(y))
            yield
        sol = _dot(x.astype(BF16), blockdiag_wide(rhs[0], rhs[1]))
        width = C_DV + C_DK
        u = [sol[:, i * width:i * width + C_DV] for i in range(2)]
        w = [sol[:, i * width + C_DV:(i + 1) * width].astype(BF16) for i in range(2)]
        return u, w, qe, qkd, kdec

    def recur(c, p, u, w, qe, qkd, kdec):
        rows = slice(c * CHUNK, (c + 1) * CHUNK)
        heads = (2 * p, 2 * p + 1)
        lanes = slice(heads[0] * C_DV, (heads[1] + 1) * C_DV)
        st = [st_ref[h] for h in heads]
        stb = [s.astype(BF16) for s in st]
        yield
        v_new = [(u[i] - _dot_nt(w[i], stb[i])).astype(BF16) for i in range(2)]
        yield
        o_intra = _dot(qkd, blockdiag_wide(v_new[0], v_new[1]))
        o = [_dot_nt(qe[i], stb[i]) + o_intra[:, i * C_DV:(i + 1) * C_DV] for i in range(2)]
        for i, h in enumerate(heads):
            a_last = shared[c]["e"][CHUNK - 1:CHUNK, SM_A + h:SM_A + h + 1]
            st_ref[h] = a_last * st[i] + _dot_tn(v_new[i], kdec[i])
        yield
        y_ref[rows, lanes] = _head_out(o, onorm_ref[...], _f32(cg_ref, rows, lanes),
                                       _f32(m_ref, rows, lanes)).astype(BF16)

    n_chunk, n_pair = tb // CHUNK, C_HEADS // 2
    pre = _run_interleaved([pair(c, p) for c in range(n_chunk) for p in range(n_pair)], GDN_GROUP)
    xb_ref[0:hist, :] = xb_ref[tb:tb + hist, :]
    for c in range(n_chunk):
        _run_interleaved([recur(c, p, *pre[c * n_pair + p]) for p in range(n_pair)], n_pair)


GDN_TB = 512
GDN_HIST = CHUNK
GDN_GROUP = 32


def _mixer_c(proj3, small3, conv, head_params, onorm):
    tb = min(GDN_TB, proj3.shape[1])
    return _mixer_call(
        _mixer_c_kernel, "gated_deltanet_mixer", proj3,
        [BLK_CQ, BLK_CK, BLK_CV, BLK_CG, BLK_M2],
        [small3, conv, head_params, onorm],
        [pl.BlockSpec((None, tb, LANES), lambda b, t: (b, t, 0)),
         _full_spec((C_CONV, C_QKV)), _full_spec((SUBLANES, LANES)), _full_spec((1, C_DV))],
        [pltpu.VMEM((C_HEADS, C_DV, C_DK), F32), pltpu.VMEM((GDN_HIST + tb, C_QKV), proj3.dtype)],
        tb=tb)


MLP_TM = 512
MLP_FF_CHUNK = 1024


def _out_mlp_kernel(x_ref, ya_ref, yb_ref, yc_ref, wo_ref, wu_ref, wd_ref, ln_ref, o_ref):
    y = (_f32(ya_ref) + _f32(yb_ref) + _f32(yc_ref)).astype(BF16)
    mix = _dot(y, wo_ref[...])
    x1 = x_ref[...] + _rms_rows(mix, ln_ref[0:1, :])
    h = _rms_rows(x1, ln_ref[1:2, :]).astype(BF16)
    down = None
    for lo in range(0, D_FF, MLP_FF_CHUNK):
        up = _dot(h, wu_ref[:, lo:lo + MLP_FF_CHUNK])
        act = jnp.square(jnp.maximum(up, 0.0)).astype(BF16)
        part = _dot(act, wd_ref[lo:lo + MLP_FF_CHUNK, :])
        down = part if down is None else down + part
    o_ref[...] = x1 + _rms_rows(down, ln_ref[2:3, :])


def _out_mlp(x2, ya, yb, yc, wo, wu, wd, ln):
    n = x2.shape[0]
    tm = min(MLP_TM, n)
    tok = pl.BlockSpec((tm, D_MODEL), lambda i: (i, 0))

    def resident(shape):
        return pl.BlockSpec(shape, lambda i: (0, 0), pipeline_mode=pl.Buffered(1))

    return pl.pallas_call(
        _out_mlp_kernel,
        grid=(n // tm,),
        in_specs=[tok, tok, tok, tok,
                  resident((D_MODEL, D_MODEL)), resident((D_MODEL, D_FF)), resident((D_FF, D_MODEL)),
                  resident((SUBLANES, D_MODEL))],
        out_specs=tok,
        out_shape=jax.ShapeDtypeStruct((n, D_MODEL), F32),
        compiler_params=pltpu.CompilerParams(
            dimension_semantics=("arbitrary",), vmem_limit_bytes=VMEM_LIMIT),
        name="merge_out_proj_mlp",
    )(x2, ya, yb, yc, wo, wu, wd, ln)


def _reorder_w_in(w_in):
    o_bgk = 4 * D_MODEL + 2 * B_KEY + D_MODEL
    o_bg = o_bgk + B_RANK
    o_cqkv = o_bg + D_MODEL
    o_ca = o_cqkv + C_QKV
    o_cg = o_ca + 2 * C_HEADS
    main = jnp.concatenate([w_in[..., :D_MODEL].astype(BF16), w_in[..., 2 * D_MODEL:o_bgk].astype(BF16),
                            w_in[..., o_bg:o_ca].astype(BF16), w_in[..., o_cg:].astype(BF16)], axis=-1)
    forget = w_in[..., D_MODEL:2 * D_MODEL]
    pad = jnp.zeros(w_in.shape[:-1] + (LANES - B_RANK - 2 * C_HEADS,), w_in.dtype)
    small = jnp.concatenate([w_in[..., o_bgk:o_bg], w_in[..., o_ca:o_cg], pad], axis=-1)
    return main, forget.astype(BF16), small.astype(F32)


def kernel(x, ln_mix_pre, ln_mix_post, ln_mlp_pre, ln_mlp_post, w_in, hgrn_lb_logits, gla_w_gk, gla_b_gk,
           gdn_conv, gdn_a_log, gdn_dt_bias, hgrn_onorm, gla_onorm, gdn_onorm, w_out, w_up, w_down):
    bsz, t_len, _ = x.shape
    n = bsz * t_len
    assert t_len % CHUNK == 0 and n % min(PROJ_TM, n) == 0 and n % min(MLP_TM, n) == 0

    lbp = _lb_params(hgrn_lb_logits)
    w_main, w_forget, w_small = _reorder_w_in(w_in)
    wgk = jnp.pad(gla_w_gk.astype(F32), ((0, 0), (0, LANES - B_RANK), (0, 0)))
    lane_pad = ((0, 0), (SM_A, LANES - SM_A - C_HEADS))
    head_params = jnp.stack([jnp.pad(gdn_a_log.astype(F32), lane_pad),
                             jnp.pad(gdn_dt_bias.astype(F32), lane_pad)], axis=1)
    head_params = jnp.pad(head_params, ((0, 0), (0, SUBLANES - 2), (0, 0)))
    ln_rest = jnp.stack([ln_mix_post, ln_mlp_pre, ln_mlp_post], axis=1).astype(F32)
    ln_rest = jnp.pad(ln_rest, ((0, 0), (0, SUBLANES - 3), (0, 0)))
    wo, wu, wd = w_out.astype(BF16), w_up.astype(BF16), w_down.astype(BF16)

    x2 = x.reshape(n, D_MODEL).astype(F32)
    for layer in range(DEPTH):
        proj, forget, small = _proj(x2, ln_mix_pre[layer][None, :].astype(F32), w_main[layer],
                                    w_forget[layer], w_small[layer])
        proj3 = proj.reshape(bsz, t_len, N_MAIN)
        small3 = small.reshape(bsz, t_len, LANES)
        ya = _mixer_a(proj3, forget.reshape(bsz, t_len, D_MODEL), lbp[layer],
                      hgrn_onorm[layer][None, :].astype(F32))
        yb = _mixer_b(proj3, small3, wgk[layer], gla_b_gk[layer][None, :].astype(F32),
                      gla_onorm[layer][None, :].astype(F32))
        yc = _mixer_c(proj3, small3, gdn_conv[layer].astype(F32), head_params[layer],
                      gdn_onorm[layer][None, :].astype(F32))
        x2 = _out_mlp(x2, ya.reshape(n, D_MODEL), yb.reshape(n, D_MODEL), yc.reshape(n, D_MODEL),
                      wo[layer], wu[layer], wd[layer], ln_rest[layer])
    return x2.reshape(bsz, t_len, D_MODEL).astype(x.dtype)
```

```python
import functools

import jax
import jax.numpy as jnp
from jax import lax
from jax.experimental import pallas as pl
from jax.experimental.pallas import tpu as pltpu

F32 = jnp.float32
BF16 = jnp.bfloat16

D_MODEL = 1024
DEPTH = 4
CHUNK = 64
CHUNK_LOG2 = CHUNK.bit_length() - 1
EPS = 1e-6
LOG2_E = 1.4426950408889634
LANES = 128
SUBLANES = 8

A_HEADS, A_DK, A_DV = 8, 128, 128
B_HEADS, B_DK, B_DV = 4, 128, 256
B_KEY = B_HEADS * B_DK
B_RANK = 16
B_GATE_NORM = 16.0
C_HEADS, C_DK, C_DV = 8, 128, 128
C_KEY = C_HEADS * C_DK
C_CONV = 4
C_QKV = 2 * C_KEY + D_MODEL
D_FF = 4 * D_MODEL

BLK_AQ, BLK_AI, BLK_AG = 0, 1, 2
BLK_BQK, BLK_BV, BLK_BG = 3, 4, 5
BLK_CQ, BLK_CK, BLK_CV, BLK_CG = 6, 7, 8, 9
BLK_M0, BLK_M1, BLK_M2 = 10, 11, 12
N_MAIN = 13 * D_MODEL
SM_GK, SM_A, SM_B = 0, B_RANK, B_RANK + C_HEADS

VMEM_LIMIT = 56 * 1024 * 1024


def _dot(a, b):
    return jnp.dot(a, b, preferred_element_type=F32)


def _dot_nt(a, b):
    return lax.dot_general(a, b, (((1,), (1,)), ((), ())), preferred_element_type=F32)


def _dot_tn(a, b):
    return lax.dot_general(a, b, (((0,), (0,)), ((), ())), preferred_element_type=F32)


def _split2(a):
    hi = a.astype(BF16)
    lo = (a - hi.astype(F32)).astype(BF16)
    return hi, lo


def _cumsum_rows(g):
    row = lax.broadcasted_iota(jnp.int32, (CHUNK, CHUNK), 0)
    col = lax.broadcasted_iota(jnp.int32, (CHUNK, CHUNK), 1)
    tril = jnp.where(row >= col, 1.0, 0.0).astype(BF16)
    g1 = g.astype(BF16)
    r1 = g - g1.astype(F32)
    g2 = r1.astype(BF16)
    g3 = (r1 - g2.astype(F32)).astype(BF16)
    return _dot(tril, g1) + _dot(tril, g2) + _dot(tril, g3)


def _sigmoid(x):
    return jax.nn.sigmoid(x)


def _silu(x):
    return x * jax.nn.sigmoid(x)


def _log1p_exp_neg_abs(x):
    return jnp.log(1.0 + jnp.exp(-jnp.abs(x)))


def _log_sigmoid(x):
    return jnp.minimum(x, 0.0) - _log1p_exp_neg_abs(x)


def _rms_rows(x, w):
    return x * lax.rsqrt(jnp.mean(x * x, axis=-1, keepdims=True) + EPS) * w


def _lb_kernel(logits_ref, out_ref):
    lg = logits_ref[...]
    e = jnp.exp(lg - jnp.max(lg, axis=0, keepdims=True))
    p = e / jnp.sum(e, axis=0, keepdims=True)
    cum = p[0:1]
    first = cum
    pad = jnp.zeros((SUBLANES - 3, lg.shape[1]), F32)
    for layer in range(DEPTH):
        if layer > 0:
            cum = cum + p[layer:layer + 1]
        lb = jnp.clip(cum - first, 0.0, 1.0)
        out_ref[layer] = jnp.concatenate([jnp.log(lb), jnp.log1p(-lb), 1.0 - lb, pad], axis=0)


def _lb_params(logits):
    return pl.pallas_call(
        _lb_kernel,
        out_shape=jax.ShapeDtypeStruct((DEPTH, SUBLANES, logits.shape[1]), F32),
        name="hgrn_lower_bounds",
    )(logits.astype(F32))


PROJ_TM = 1024
PROJ_TN = N_MAIN // 8


def _proj_kernel(x_ref, lnw_ref, w_ref, wf_ref, ws_ref, o_ref, of_ref, os_ref, h_ref):
    @pl.when(pl.program_id(1) == 0)
    def _():
        h = _rms_rows(x_ref[...], lnw_ref[...])
        hh, hl = _split2(h)
        h_ref[...] = hh
        wh, wl = _split2(ws_ref[...])
        os_ref[...] = _dot(hh, wh) + _dot(hh, wl) + _dot(hl, wh)
        of_ref[...] = _dot(hh, wf_ref[...])

    o_ref[...] = _dot(h_ref[...], w_ref[...]).astype(BF16)


def _proj(x2, lnw, w_main, w_forget, w_small):
    n = x2.shape[0]
    tm = min(PROJ_TM, n)

    def resident(shape):
        return pl.BlockSpec(shape, lambda i, j: (0, 0), pipeline_mode=pl.Buffered(1))

    return pl.pallas_call(
        _proj_kernel,
        grid=(n // tm, N_MAIN // PROJ_TN),
        in_specs=[
            pl.BlockSpec((tm, D_MODEL), lambda i, j: (i, 0)),
            resident((1, D_MODEL)),
            pl.BlockSpec((D_MODEL, PROJ_TN), lambda i, j: (0, j)),
            resident((D_MODEL, D_MODEL)),
            resident((D_MODEL, LANES)),
        ],
        out_specs=[
            pl.BlockSpec((tm, PROJ_TN), lambda i, j: (i, j)),
            pl.BlockSpec((tm, D_MODEL), lambda i, j: (i, 0)),
            pl.BlockSpec((tm, LANES), lambda i, j: (i, 0)),
        ],
        out_shape=[
            jax.ShapeDtypeStruct((n, N_MAIN), BF16),
            jax.ShapeDtypeStruct((n, D_MODEL), F32),
            jax.ShapeDtypeStruct((n, LANES), F32),
        ],
        scratch_shapes=[pltpu.VMEM((tm, D_MODEL), BF16)],
        compiler_params=pltpu.CompilerParams(
            dimension_semantics=("arbitrary", "arbitrary"), vmem_limit_bytes=VMEM_LIMIT),
        name="rmsnorm_in_proj",
    )(x2, lnw, w_main, w_forget, w_small)


def _chunk_masks():
    row = lax.broadcasted_iota(jnp.int32, (CHUNK, CHUNK), 0)
    col = lax.broadcasted_iota(jnp.int32, (CHUNK, CHUNK), 1)
    sibling = []
    for lvl in range(6):
        bi, bj = row >> lvl, col >> lvl
        sibling.append((bi == bj + 1) & ((bi & 1) == 1))
    return row == col, sibling


def _run_interleaved(gens, group):
    results = [None] * len(gens)
    for lo in range(0, len(gens), group):
        live = list(enumerate(gens))[lo:lo + group]
        while live:
            still = []
            for i, gen in live:
                try:
                    next(gen)
                    still.append((i, gen))
                except StopIteration as stop:
                    results[i] = stop.value
            live = still
    return results


def _gla_head(q, k, v, g, st_ref, kb_ref, h, masks):
    eye, sibling = masks
    width = q.shape[1]
    g = g * LOG2_E
    b = _cumsum_rows(g)
    yield
    b_last = b[CHUNK - 1:CHUNK]
    qd = (q * jnp.exp2(b)).astype(BF16)
    kd = (k * jnp.exp2(b_last - b)).astype(BF16)
    s_decay = jnp.exp2(b_last)
    vb = v.astype(BF16)
    qb = q.astype(BF16)
    kb = k.astype(BF16)

    kb_ref[0] = b - g
    kb_ref[1] = b

    subl = lax.broadcasted_iota(jnp.int32, (SUBLANES, width), 0)

    def row_bcast(idx, r, n):
        return jnp.broadcast_to(kb_ref[idx, pl.ds(r, 1), :], (n, width))

    half = SUBLANES // 2

    def half_group_rows(idx, off):
        return jnp.concatenate(
            [jnp.where(subl < half, row_bcast(idx, base + off, SUBLANES), row_bcast(idx, base + half + off, SUBLANES))
             for base in range(0, CHUNK, SUBLANES)], axis=0)

    q_lv = [(q * jnp.exp2(g)).astype(BF16)]
    k_lv = [kb]
    odd = (subl & 1) == 1
    g_prev, g_next = [], []
    for base in range(0, CHUNK, SUBLANES):
        piece = g[base:base + SUBLANES]
        g_prev.append(jnp.where(odd, pltpu.roll(piece, 1, 0), 0.0))
        g_next.append(jnp.where(odd, 0.0, pltpu.roll(piece, SUBLANES - 1, 0)))
    q_lv.append((q * jnp.exp2(g + jnp.concatenate(g_prev, axis=0))).astype(BF16))
    k_lv.append((k * jnp.exp2(jnp.concatenate(g_next, axis=0))).astype(BF16))
    q_lv.append((q * jnp.exp2(b - half_group_rows(0, 0))).astype(BF16))
    k_lv.append((k * jnp.exp2(half_group_rows(1, half - 1) - b)).astype(BF16))
    unit = 2 * SUBLANES
    later_rows = {}
    for s in (8, 16, 32):
        later_rows[s] = [lo for lo in range(0, CHUNK, SUBLANES) if (lo // s) & 1]
        q_rows = [q[lo:lo + SUBLANES] * jnp.exp2(b[lo:lo + SUBLANES] - row_bcast(0, lo // s * s, SUBLANES))
                  for lo in later_rows[s]]
        q_lv.append(jnp.concatenate(q_rows, axis=0).astype(BF16))
        k_units = []
        for lo in range(0, CHUNK, unit):
            if s == SUBLANES:
                early = slice(lo, lo + SUBLANES)
                scaled = k[early] * jnp.exp2(row_bcast(1, lo + SUBLANES - 1, SUBLANES) - b[early])
                k_units.append(jnp.concatenate([scaled, k[lo + SUBLANES:lo + unit]], axis=0).astype(BF16))
            elif (lo // s) & 1:
                k_units.append(kb[lo:lo + unit])
            else:
                rows = slice(lo, lo + unit)
                k_units.append((k[rows] * jnp.exp2(row_bcast(1, lo // s * s + s - 1, unit) - b[rows])).astype(BF16))
        k_lv.append(jnp.concatenate(k_units, axis=0))

    yield
    first = _dot_nt(jnp.concatenate([q_lv[0], qb], axis=0), kb)
    prods = [_dot_nt(q_lv[lvl], k_lv[lvl]) for lvl in range(1, 6)]
    st = st_ref[h]
    o_inter = _dot_nt(qd, st.astype(BF16))
    st_ref[h] = st * s_decay + _dot_tn(vb, kd)
    yield
    a = jnp.where(sibling[0], first[:CHUNK], jnp.where(eye, first[CHUNK:], 0.0))
    for lvl in (1, 2):
        a = jnp.where(sibling[lvl], prods[lvl - 1], a)
    a_rows = [a[lo:lo + SUBLANES] for lo in range(0, CHUNK, SUBLANES)]
    for lvl in (3, 4, 5):
        for i, lo in enumerate(later_rows[1 << lvl]):
            piece = prods[lvl - 1][i * SUBLANES:(i + 1) * SUBLANES]
            a_rows[lo // SUBLANES] = jnp.where(sibling[lvl][lo:lo + SUBLANES], piece, a_rows[lo // SUBLANES])
    ab = jnp.concatenate(a_rows, axis=0).astype(BF16)
    yield
    return o_inter + _dot(ab, vb)


def _head_out(outs, onorm, gate, merge):
    y = jnp.concatenate([_rms_rows(o, onorm) for o in outs], axis=-1)
    return y * _silu(gate) * _sigmoid(merge)


def _f32(ref, rows=slice(None), lanes=slice(None)):
    return ref[rows, lanes].astype(F32)


def _mixer_a_kernel(aq_ref, ai_ref, ag_ref, m_ref, af_ref, lbp_ref, onorm_ref, y_ref, st_ref, kb_ref):
    @pl.when(pl.program_id(1) == 0)
    def _():
        st_ref[...] = jnp.zeros_like(st_ref)

    masks = _chunk_masks()

    def head(c, h):
        rows = slice(c * CHUNK, (c + 1) * CHUNK)
        hs = slice(h * A_DK, (h + 1) * A_DK)
        z = af_ref[rows, hs]
        log_lb, log_1m_lb, one_m_lb = lbp_ref[0:1, hs], lbp_ref[1:2, hs], lbp_ref[2:3, hs]
        t = log_1m_lb + _log_sigmoid(z)
        g = jnp.maximum(log_lb, t) + _log1p_exp_neg_abs(log_lb - t)
        k = one_m_lb * _sigmoid(-z)
        q = _silu(_f32(aq_ref, rows, hs))
        o = yield from _gla_head(q, k, ai_ref[rows, hs], g, st_ref, kb_ref.at[c * A_HEADS + h], h, masks)
        yield
        y = _head_out([o], onorm_ref[...], _f32(ag_ref, rows, hs), _f32(m_ref, rows, hs))
        y_ref[rows, hs] = y.astype(BF16)

    _run_interleaved([head(c, h) for c in range(aq_ref.shape[0] // CHUNK) for h in range(A_HEADS)],
                     GLA_GROUP)


def _mixer_b_kernel(qk_ref, v_ref, bg_ref, m_ref, sm_ref, wgk_ref, bgk_ref, onorm_ref, y_ref, st_ref, kb_ref):
    @pl.when(pl.program_id(1) == 0)
    def _():
        st_ref[...] = jnp.zeros_like(st_ref)

    masks = _chunk_masks()

    def head(c, h):
        rows = slice(c * CHUNK, (c + 1) * CHUNK)
        hs = slice(h * B_DK, (h + 1) * B_DK)
        vs = slice(h * B_DV, (h + 1) * B_DV)
        q = _f32(qk_ref, rows, hs) * (B_DK ** -0.5)
        k = _f32(qk_ref, rows, slice(B_KEY + h * B_DK, B_KEY + (h + 1) * B_DK))
        smh, sml = _split2(sm_ref[rows, :])
        wh, wl = _split2(wgk_ref[:, hs])
        gk = _dot(smh, wh) + _dot(smh, wl) + _dot(sml, wh) + bgk_ref[:, hs]
        yield
        g = _log_sigmoid(gk) / B_GATE_NORM
        o = yield from _gla_head(q, k, v_ref[rows, vs], g, st_ref, kb_ref.at[c * B_HEADS + h], h, masks)
        yield
        y = _head_out([o], onorm_ref[...], _f32(bg_ref, rows, vs), _f32(m_ref, rows, vs))
        y_ref[rows, vs] = y.astype(BF16)

    _run_interleaved([head(c, h) for c in range(qk_ref.shape[0] // CHUNK) for h in range(B_HEADS)],
                     GLA_GROUP // 2)


def _tok_index(b, t, blk):
    return (b, t, blk)


def _full_spec(shape):
    return pl.BlockSpec(shape, lambda b, t: (0,) * len(shape))


def _mixer_call(kernel, name, proj3, tok_blocks, extra_inputs, extra_specs, scratch, tb=CHUNK):
    bsz, t_len, _ = proj3.shape
    assert t_len % tb == 0 and tb % CHUNK == 0
    return pl.pallas_call(
        kernel,
        grid=(bsz, t_len // tb),
        in_specs=[pl.BlockSpec((None, tb, D_MODEL), functools.partial(_tok_index, blk=blk))
                  for blk in tok_blocks] + extra_specs,
        out_specs=pl.BlockSpec((None, tb, D_MODEL), lambda b, t: (b, t, 0)),
        out_shape=jax.ShapeDtypeStruct((bsz, t_len, D_MODEL), BF16),
        scratch_shapes=scratch,
        compiler_params=pltpu.CompilerParams(
            dimension_semantics=("arbitrary", "arbitrary"), vmem_limit_bytes=VMEM_LIMIT),
        name=name,
    )(*([proj3] * len(tok_blocks)), *extra_inputs)


GLA_TB = 256
GLA_GROUP = 32


def _mixer_a(proj3, forget3, lbp, onorm):
    tb = min(GLA_TB, proj3.shape[1])
    return _mixer_call(
        _mixer_a_kernel, "hgrn2_mixer", proj3,
        [BLK_AQ, BLK_AI, BLK_AG, BLK_M0],
        [forget3, lbp, onorm],
        [pl.BlockSpec((None, tb, D_MODEL), lambda b, t: (b, t, 0)),
         _full_spec((SUBLANES, D_MODEL)), _full_spec((1, A_DV))],
        [pltpu.VMEM((A_HEADS, A_DV, A_DK), F32),
         pltpu.VMEM((tb // CHUNK * A_HEADS, 2, CHUNK, A_DK), F32)],
        tb=tb)


def _mixer_b(proj3, small3, wgk, bgk, onorm):
    tb = min(2 * GLA_TB, proj3.shape[1])
    return _mixer_call(
        _mixer_b_kernel, "gla_mixer", proj3,
        [BLK_BQK, BLK_BV, BLK_BG, BLK_M1],
        [small3, wgk, bgk, onorm],
        [pl.BlockSpec((None, tb, LANES), lambda b, t: (b, t, 0)),
         _full_spec((LANES, B_KEY)), _full_spec((1, B_KEY)), _full_spec((1, B_DV))],
        [pltpu.VMEM((B_HEADS, B_DV, B_DK), F32),
         pltpu.VMEM((tb // CHUNK * B_HEADS, 2, CHUNK, B_DK), F32)],
        tb=tb)


def _l2norm_rows(x):
    return x * lax.rsqrt(jnp.sum(x * x, axis=-1, keepdims=True) + EPS)


def _mixer_c_kernel(cq_ref, ck_ref, cv_ref, cg_ref, m_ref, sm_ref, conv_ref, hp_ref, onorm_ref,
                    y_ref, st_ref, xb_ref):
    hist = GDN_HIST

    @pl.when(pl.program_id(1) == 0)
    def _():
        st_ref[...] = jnp.zeros_like(st_ref)
        xb_ref[0:hist, :] = jnp.zeros((hist, C_QKV), xb_ref.dtype)

    tb = cq_ref.shape[0]
    xb_ref[hist:hist + tb, 0:C_KEY] = cq_ref[...]
    xb_ref[hist:hist + tb, C_KEY:2 * C_KEY] = ck_ref[...]
    xb_ref[hist:hist + tb, 2 * C_KEY:C_QKV] = cv_ref[...]

    row = lax.broadcasted_iota(jnp.int32, (CHUNK, 2 * CHUNK), 0)
    lane = lax.broadcasted_iota(jnp.int32, (CHUNK, 2 * CHUNK), 1)
    col = lane & (CHUNK - 1)
    even = lane < CHUNK
    eye = jnp.where(row == col, 1.0, 0.0).astype(F32)
    causal = row >= col
    merge_mask = []
    for lvl in range(6):
        bi, bj = row >> lvl, col >> lvl
        merge_mask.append((bi == bj + 1) & ((bi & 1) == 1))
    keep_even = jnp.where(even, 1.0, 0.0).astype(BF16)
    keep_odd = jnp.where(even, 0.0, 1.0).astype(BF16)

    def blockdiag(m):
        return jnp.concatenate([m * keep_even, m * keep_odd], axis=0)

    def blockdiag_wide(m0, m1):
        z0, z1 = jnp.zeros_like(m1), jnp.zeros_like(m0)
        return jnp.concatenate([jnp.concatenate([m0, z0], axis=1), jnp.concatenate([z1, m1], axis=1)], axis=0)

    n_delay = C_CONV - 1
    dr = lax.broadcasted_iota(jnp.int32, (n_delay * CHUNK, hist + CHUNK), 0)
    dc = lax.broadcasted_iota(jnp.int32, (n_delay * CHUNK, hist + CHUNK), 1)
    delay_sel = jnp.where(dc == hist + (dr & (CHUNK - 1)) - ((dr >> CHUNK_LOG2) + 1), 1.0, 0.0)
    delay_sel = delay_sel.astype(xb_ref.dtype)

    def conv_silu(c, lanes):
        window = xb_ref[c * CHUNK:c * CHUNK + hist + CHUNK, lanes]
        delayed = _dot(delay_sel, window)
        acc = window[hist:].astype(F32) * conv_ref[C_CONV - 1:C_CONV, lanes]
        for j in range(1, C_CONV):
            acc = acc + delayed[(j - 1) * CHUNK:j * CHUNK] * conv_ref[C_CONV - 1 - j:C_CONV - j, lanes]
        return _silu(acc)

    shared = []
    for c in range(tb // CHUNK):
        sm = sm_ref[c * CHUNK:(c + 1) * CHUNK, :]
        log2_a = (-LOG2_E) * jnp.exp(hp_ref[0:1, :]) * jax.nn.softplus(sm + hp_ref[1:2, :])
        b_all = _cumsum_rows(log2_a)
        shared.append(dict(
            beta=_sigmoid(sm), b=b_all, b_t=jnp.concatenate([b_all, b_all], axis=0).T, e=jnp.exp2(b_all),
            d=jnp.exp2(b_all[CHUNK - 1:CHUNK] - b_all)))

    def pair(c, p):
        sh = shared[c]
        heads = (2 * p, 2 * p + 1)
        q, k, rhs, qe, kdec, beta, b_col = [], [], [], [], [], [], []
        for h in heads:
            q_h = _l2norm_rows(conv_silu(c, slice(h * C_DK, (h + 1) * C_DK))) * (C_DK ** -0.5)
            k_h = _l2norm_rows(conv_silu(c, slice(C_KEY + h * C_DK, C_KEY + (h + 1) * C_DK)))
            v_h = conv_silu(c, slice(2 * C_KEY + h * C_DV, 2 * C_KEY + (h + 1) * C_DV))
            beta_h = sh["beta"][:, SM_B + h:SM_B + h + 1]
            e_b = sh["e"][:, SM_A + h:SM_A + h + 1]
            q.append(q_h.astype(BF16))
            k.append(k_h.astype(BF16))
            rhs.append(jnp.concatenate([v_h * beta_h, k_h * (beta_h * e_b)], axis=-1).astype(BF16))
            qe.append((q_h * e_b).astype(BF16))
            kdec.append((k_h * sh["d"][:, SM_A + h:SM_A + h + 1]).astype(BF16))
            beta.append(beta_h)
            b_col.append(sh["b"][:, SM_A + h:SM_A + h + 1])
        yield
        b_row = jnp.where(even[0:1], sh["b_t"][SM_A + heads[0]:SM_A + heads[0] + 1, :],
                          sh["b_t"][SM_A + heads[1]:SM_A + heads[1] + 1, :])
        decay = jnp.where(causal, jnp.exp2(jnp.minimum(jnp.where(even, b_col[0], b_col[1]) - b_row, 0.0)), 0.0)
        k_diag = blockdiag_wide(k[0], k[1])
        kk = _dot_nt(jnp.concatenate(k, axis=1), k_diag)
        qk = _dot_nt(jnp.concatenate(q, axis=1), k_diag)
        yield
        lower = jnp.where(even, beta[0], beta[1]) * kk * decay
        qkd = (qk * decay).astype(BF16)
        x = eye - jnp.where(merge_mask[0], lower, 0.0)
        for lvl in range(1, 6):
            cpart = jnp.where(merge_mask[lvl], lower, 0.0).astype(BF16)
            xb = x.astype(BF16)
            y = _dot(cpart, blockdiag(xb)).astype(BF16)
            yield
            x = x - _dot(xb, blockdiag(y))
            yield
        sol = _dot(x.astype(BF16), blockdiag_wide(rhs[0], rhs[1]))
        width = C_DV + C_DK
        u = [sol[:, i * width:i * width + C_DV] for i in range(2)]
        w = [sol[:, i * width + C_DV:(i + 1) * width].astype(BF16) for i in range(2)]
        return u, w, qe, qkd, kdec

    def recur(c, p, u, w, qe, qkd, kdec):
        rows = slice(c * CHUNK, (c + 1) * CHUNK)
        heads = (2 * p, 2 * p + 1)
        lanes = slice(heads[0] * C_DV, (heads[1] + 1) * C_DV)
        st = [st_ref[h] for h in heads]
        stb = [s.astype(BF16) for s in st]
        yield
        v_new = [(u[i] - _dot_nt(w[i], stb[i])).astype(BF16) for i in range(2)]
        yield
        o_intra = _dot(qkd, blockdiag_wide(v_new[0], v_new[1]))
        o = [_dot_nt(qe[i], stb[i]) + o_intra[:, i * C_DV:(i + 1) * C_DV] for i in range(2)]
        for i, h in enumerate(heads):
            a_last = shared[c]["e"][CHUNK - 1:CHUNK, SM_A + h:SM_A + h + 1]
            st_ref[h] = a_last * st[i] + _dot_tn(v_new[i], kdec[i])
        yield
        y_ref[rows, lanes] = _head_out(o, onorm_ref[...], _f32(cg_ref, rows, lanes),
                                       _f32(m_ref, rows, lanes)).astype(BF16)

    n_chunk, n_pair = tb // CHUNK, C_HEADS // 2
    pre = _run_interleaved([pair(c, p) for c in range(n_chunk) for p in range(n_pair)], GDN_GROUP)
    xb_ref[0:hist, :] = xb_ref[tb:tb + hist, :]
    for c in range(n_chunk):
        _run_interleaved([recur(c, p, *pre[c * n_pair + p]) for p in range(n_pair)], n_pair)


GDN_TB = 256
GDN_HIST = CHUNK
GDN_GROUP = 16


def _mixer_c(proj3, small3, conv, head_params, onorm):
    tb = min(GDN_TB, proj3.shape[1])
    return _mixer_call(
        _mixer_c_kernel, "gated_deltanet_mixer", proj3,
        [BLK_CQ, BLK_CK, BLK_CV, BLK_CG, BLK_M2],
        [small3, conv, head_params, onorm],
        [pl.BlockSpec((None, tb, LANES), lambda b, t: (b, t, 0)),
         _full_spec((C_CONV, C_QKV)), _full_spec((SUBLANES, LANES)), _full_spec((1, C_DV))],
        [pltpu.VMEM((C_HEADS, C_DV, C_DK), F32), pltpu.VMEM((GDN_HIST + tb, C_QKV), proj3.dtype)],
        tb=tb)


MLP_TM = 512
MLP_FF_CHUNK = 1024


def _out_mlp_kernel(x_ref, ya_ref, yb_ref, yc_ref, wo_ref, wu_ref, wd_ref, ln_ref, o_ref):
    y = (_f32(ya_ref) + _f32(yb_ref) + _f32(yc_ref)).astype(BF16)
    mix = _dot(y, wo_ref[...])
    x1 = x_ref[...] + _rms_rows(mix, ln_ref[0:1, :])
    h = _rms_rows(x1, ln_ref[1:2, :]).astype(BF16)
    down = None
    for lo in range(0, D_FF, MLP_FF_CHUNK):
        up = _dot(h, wu_ref[:, lo:lo + MLP_FF_CHUNK])
        act = jnp.square(jnp.maximum(up, 0.0)).astype(BF16)
        part = _dot(act, wd_ref[lo:lo + MLP_FF_CHUNK, :])
        down = part if down is None else down + part
    o_ref[...] = x1 + _rms_rows(down, ln_ref[2:3, :])


def _out_mlp(x2, ya, yb, yc, wo, wu, wd, ln):
    n = x2.shape[0]
    tm = min(MLP_TM, n)
    tok = pl.BlockSpec((tm, D_MODEL), lambda i: (i, 0))

    def resident(shape):
        return pl.BlockSpec(shape, lambda i: (0, 0), pipeline_mode=pl.Buffered(1))

    return pl.pallas_call(
        _out_mlp_kernel,
        grid=(n // tm,),
        in_specs=[tok, tok, tok, tok,
                  resident((D_MODEL, D_MODEL)), resident((D_MODEL, D_FF)), resident((D_FF, D_MODEL)),
                  resident((SUBLANES, D_MODEL))],
        out_specs=tok,
        out_shape=jax.ShapeDtypeStruct((n, D_MODEL), F32),
        compiler_params=pltpu.CompilerParams(
            dimension_semantics=("arbitrary",), vmem_limit_bytes=VMEM_LIMIT),
        name="merge_out_proj_mlp",
    )(x2, ya, yb, yc, wo, wu, wd, ln)


def _reorder_w_in(w_in):
    o_bgk = 4 * D_MODEL + 2 * B_KEY + D_MODEL
    o_bg = o_bgk + B_RANK
    o_cqkv = o_bg + D_MODEL
    o_ca = o_cqkv + C_QKV
    o_cg = o_ca + 2 * C_HEADS
    main = jnp.concatenate([w_in[..., :D_MODEL].astype(BF16), w_in[..., 2 * D_MODEL:o_bgk].astype(BF16),
                            w_in[..., o_bg:o_ca].astype(BF16), w_in[..., o_cg:].astype(BF16)], axis=-1)
    forget = w_in[..., D_MODEL:2 * D_MODEL]
    pad = jnp.zeros(w_in.shape[:-1] + (LANES - B_RANK - 2 * C_HEADS,), w_in.dtype)
    small = jnp.concatenate([w_in[..., o_bgk:o_bg], w_in[..., o_ca:o_cg], pad], axis=-1)
    return main, forget.astype(BF16), small.astype(F32)


def kernel(x, ln_mix_pre, ln_mix_post, ln_mlp_pre, ln_mlp_post, w_in, hgrn_lb_logits, gla_w_gk, gla_b_gk,
           gdn_conv, gdn_a_log, gdn_dt_bias, hgrn_onorm, gla_onorm, gdn_onorm, w_out, w_up, w_down):
    bsz, t_len, _ = x.shape
    n = bsz * t_len
    assert t_len % CHUNK == 0 and n % min(PROJ_TM, n) == 0 and n % min(MLP_TM, n) == 0

    lbp = _lb_params(hgrn_lb_logits)
    w_main, w_forget, w_small = _reorder_w_in(w_in)
    wgk = jnp.pad(gla_w_gk.astype(F32), ((0, 0), (0, LANES - B_RANK), (0, 0)))
    lane_pad = ((0, 0), (SM_A, LANES - SM_A - C_HEADS))
    head_params = jnp.stack([jnp.pad(gdn_a_log.astype(F32), lane_pad),
                             jnp.pad(gdn_dt_bias.astype(F32), lane_pad)], axis=1)
    head_params = jnp.pad(head_params, ((0, 0), (0, SUBLANES - 2), (0, 0)))
    ln_rest = jnp.stack([ln_mix_post, ln_mlp_pre, ln_mlp_post], axis=1).astype(F32)
    ln_rest = jnp.pad(ln_rest, ((0, 0), (0, SUBLANES - 3), (0, 0)))
    wo, wu, wd = w_out.astype(BF16), w_up.astype(BF16), w_down.astype(BF16)

    x2 = x.reshape(n, D_MODEL).astype(F32)
    for layer in range(DEPTH):
        proj, forget, small = _proj(x2, ln_mix_pre[layer][None, :].astype(F32), w_main[layer],
                                    w_forget[layer], w_small[layer])
        proj3 = proj.reshape(bsz, t_len, N_MAIN)
        small3 = small.reshape(bsz, t_len, LANES)
        ya = _mixer_a(proj3, forget.reshape(bsz, t_len, D_MODEL), lbp[layer],
                      hgrn_onorm[layer][None, :].astype(F32))
        yb = _mixer_b(proj3, small3, wgk[layer], gla_b_gk[layer][None, :].astype(F32),
                      gla_onorm[layer][None, :].astype(F32))
        yc = _mixer_c(proj3, small3, gdn_conv[layer].astype(F32), head_params[layer],
                      gdn_onorm[layer][None, :].astype(F32))
        x2 = _out_mlp(x2, ya.reshape(n, D_MODEL), yb.reshape(n, D_MODEL), yc.reshape(n, D_MODEL),
                      wo[layer], wu[layer], wd[layer], ln_rest[layer])
    return x2.reshape(bsz, t_len, D_MODEL).astype(x.dtype)
```
